```python
import jax, jax.numpy as jnp
from jax import lax
import numpy as np

D_MODEL = 1024
BATCH = 16
SEQ = 4096
DEPTH = 1
DEC_BATCH = 8
DEC_SEQ = 16
PAST_LEN = 1024

CHUNK = 64
HEAD_DIM = 64
A_Q_HEADS = 8
A_KV_HEADS = 2
A_WINDOW = 128
A_PREV_CHUNKS = A_WINDOW // CHUNK
B_HEADS = 8
B_PREV_CHUNKS = 8
B_REACH = B_PREV_CHUNKS * CHUNK
REL_CLIP = 128
ROPE_THETA = 10000.0
D_FF = 2816
CONV_WIDTH = 3
RMS_EPS = 1e-6
NEG_INF = -1e30
A_Q_W = A_Q_HEADS * HEAD_DIM
A_KV_W = A_KV_HEADS * HEAD_DIM
B_W = B_HEADS * HEAD_DIM
IN_SPLITS = (A_Q_W, A_KV_W, A_KV_W, B_W, B_W, B_W, D_MODEL, D_MODEL)
IN_WIDTH = A_Q_W + 2 * A_KV_W + 3 * B_W + 2 * D_MODEL

kernel_name = "hybrid_stream_swa_sink_chunkrel_convglu"


def rms_norm(x, g):
    xf = x.astype(jnp.float32)
    y = xf * lax.rsqrt(jnp.mean(xf * xf, axis=-1, keepdims=True) + RMS_EPS)
    return (y * g.astype(jnp.float32)).astype(x.dtype)


def rope(x, pos):
    half = HEAD_DIM // 2
    inv = 1.0 / (ROPE_THETA ** (jnp.arange(half, dtype=jnp.float32) * (2.0 / HEAD_DIM)))
    ang = pos.astype(jnp.float32)[:, None] * inv[None, :]
    cos = jnp.cos(ang)[:, None, :]
    sin = jnp.sin(ang)[:, None, :]
    xf = x.astype(jnp.float32)
    x1, x2 = xf[..., :half], xf[..., half:]
    return jnp.concatenate([x1 * cos - x2 * sin, x2 * cos + x1 * sin], axis=-1).astype(x.dtype)


def attend(q, k, v, valid, rel, sink, rel_table):
    b, lq, hq, hd = q.shape
    hkv = k.shape[2]
    r = hq // hkv
    qg = q.reshape(b, lq, hkv, r, hd)
    s = jnp.einsum("bqgrd,bkgd->bgrqk", qg, k, preferred_element_type=jnp.float32) * (hd ** -0.5)
    if rel_table is not None:
        idx = jnp.clip(rel, -REL_CLIP, REL_CLIP) + REL_CLIP
        bias = jnp.take(rel_table.astype(jnp.float32), idx, axis=1)
        s = s + bias.reshape(hkv, r, lq, -1)[None]
    if valid is not None:
        s = jnp.where(valid, s, NEG_INF)
    if sink is not None:
        sk = jnp.broadcast_to(sink.astype(jnp.float32).reshape(1, hkv, r, 1, 1), s.shape[:-1] + (1,))
        p = jax.nn.softmax(jnp.concatenate([s, sk], axis=-1), axis=-1)[..., :-1]
    else:
        p = jax.nn.softmax(s, axis=-1)
    o = jnp.einsum("bgrqk,bkgd->bqgrd", p.astype(v.dtype), v)
    return o.reshape(b, lq, hq, hd)


def chunk_band_attention(q, k, v, n_prev, sink=None, rel_table=None):
    b, s_len, hq, hd = q.shape
    nc = s_len // CHUNK
    pad = n_prev * CHUNK
    band = pad + CHUNK
    kp = jnp.pad(k, ((0, 0), (pad, 0), (0, 0), (0, 0)))
    vp = jnp.pad(v, ((0, 0), (pad, 0), (0, 0), (0, 0)))
    kpos = jnp.arange(band) - pad
    rel = kpos[None, :] - jnp.arange(CHUNK)[:, None]
    qc = jnp.moveaxis(q.reshape(b, nc, CHUNK, hq, hd), 1, 0)

    def one_chunk(args):
        c, qb = args
        start = c * CHUNK
        kb = lax.dynamic_slice_in_dim(kp, start, band, axis=1)
        vb = lax.dynamic_slice_in_dim(vp, start, band, axis=1)
        valid = (start + kpos) >= 0
        return attend(qb, kb, vb, valid, rel, sink, rel_table)

    out = lax.map(one_chunk, (jnp.arange(nc), qc))
    return jnp.moveaxis(out, 0, 1).reshape(b, s_len, hq, hd)


def layer(x, start, cache_a_k, cache_a_v, cache_b_k, cache_b_v, conv_state,
          g_mix_pre, w_in, sinks, rel_bias, w_oa, w_ob, w_out, g_mix_post,
          g_ffn_pre, w_up, conv_w, conv_b, w_down, g_ffn_post):
    bsz, t, _ = x.shape
    h = rms_norm(x, g_mix_pre)
    offs = [int(o) for o in np.cumsum(IN_SPLITS)[:-1]]
    qa, ka, va, qb, kb, vb, ga, gb = jnp.split(h @ w_in, offs, axis=-1)
    qa = qa.reshape(bsz, t, A_Q_HEADS, HEAD_DIM)
    ka = ka.reshape(bsz, t, A_KV_HEADS, HEAD_DIM)
    va = va.reshape(bsz, t, A_KV_HEADS, HEAD_DIM)
    qb = qb.reshape(bsz, t, B_HEADS, HEAD_DIM)
    kb = kb.reshape(bsz, t, B_HEADS, HEAD_DIM)
    vb = vb.reshape(bsz, t, B_HEADS, HEAD_DIM)
    pos = start + jnp.arange(t)
    qa = rope(qa, pos)
    ka = rope(ka, pos)
    if cache_a_k is None:
        oa = chunk_band_attention(qa, ka, va, A_PREV_CHUNKS, sink=sinks)
        ob = chunk_band_attention(qb, kb, vb, B_PREV_CHUNKS, rel_table=rel_bias)
        keep_a = min(A_WINDOW, t)
        keep_b = min(B_REACH, t)
        new_rows = (ka[:, t - keep_a:], va[:, t - keep_a:], kb[:, t - keep_b:], vb[:, t - keep_b:])
        conv_prev = jnp.zeros((bsz, CONV_WIDTH - 1, D_FF), x.dtype)
    else:
        oa = attend(qa, jnp.concatenate([cache_a_k, ka], axis=1),
                    jnp.concatenate([cache_a_v, va], axis=1), None, None, sinks, None)
        lb = cache_b_k.shape[1]
        kpos = jnp.concatenate([jnp.arange(lb) - lb, jnp.arange(t)])
        rel = kpos[None, :] - jnp.arange(t)[:, None]
        ob = attend(qb, jnp.concatenate([cache_b_k, kb], axis=1),
                    jnp.concatenate([cache_b_v, vb], axis=1), None, rel, None, rel_bias)
        new_rows = (ka, va, kb, vb)
        conv_prev = conv_state.astype(x.dtype)
    mixed = (jax.nn.sigmoid(ga) * (oa.reshape(bsz, t, A_Q_W) @ w_oa)
             + jax.nn.sigmoid(gb) * (ob.reshape(bsz, t, B_W) @ w_ob))
    x = x + rms_norm(mixed @ w_out, g_mix_post)
    h = rms_norm(x, g_ffn_pre)
    u, val = jnp.split(h @ w_up, 2, axis=-1)
    up = jnp.concatenate([conv_prev, u], axis=1)
    conv = conv_b
    for j in range(CONV_WIDTH):
        conv = conv + up[:, j:j + t] * conv_w[j]
    new_conv = up[:, t:]
    ff = (jax.nn.gelu(conv, approximate=False) * val) @ w_down
    x = x + rms_norm(ff, g_ffn_post)
    return x, new_rows + (new_conv,)


def setup_inputs(seed: int = 0) -> dict:
    key = jax.random.key(seed)
    ks = jax.random.split(key, 24)

    def nrm(k, shape, scale):
        return jax.random.normal(k, shape, jnp.float32) * scale

    a_rows = min(A_WINDOW, PAST_LEN)
    b_rows = min(B_REACH, PAST_LEN)
    return {
        "x_prompt": nrm(ks[0], (BATCH, SEQ, D_MODEL), 1.0),
        "x_sample": nrm(ks[1], (DEC_BATCH, DEC_SEQ, D_MODEL), 1.0),
        "cache_a_k": nrm(ks[2], (DEPTH, DEC_BATCH, a_rows, A_KV_HEADS, HEAD_DIM), 1.0),
        "cache_a_v": nrm(ks[3], (DEPTH, DEC_BATCH, a_rows, A_KV_HEADS, HEAD_DIM), 1.0),
        "cache_b_k": nrm(ks[4], (DEPTH, DEC_BATCH, b_rows, B_HEADS, HEAD_DIM), 1.0),
        "cache_b_v": nrm(ks[5], (DEPTH, DEC_BATCH, b_rows, B_HEADS, HEAD_DIM), 1.0),
        "state_conv": nrm(ks[6], (DEPTH, DEC_BATCH, CONV_WIDTH - 1, D_FF), 1.0),
        "g_mix_pre": 1.0 + nrm(ks[7], (DEPTH, D_MODEL), 0.1),
        "w_in": nrm(ks[8], (DEPTH, D_MODEL, IN_WIDTH), D_MODEL ** -0.5),
        "sinks": nrm(ks[9], (DEPTH, A_Q_HEADS), 1.0),
        "rel_bias": nrm(ks[10], (DEPTH, B_HEADS, 2 * REL_CLIP + 1), 0.5),
        "w_oa": nrm(ks[11], (DEPTH, A_Q_W, D_MODEL), A_Q_W ** -0.5),
        "w_ob": nrm(ks[12], (DEPTH, B_W, D_MODEL), B_W ** -0.5),
        "w_out": nrm(ks[13], (DEPTH, D_MODEL, D_MODEL), D_MODEL ** -0.5),
        "g_mix_post": 1.0 + nrm(ks[14], (DEPTH, D_MODEL), 0.1),
        "g_ffn_pre": 1.0 + nrm(ks[15], (DEPTH, D_MODEL), 0.1),
        "w_up": nrm(ks[16], (DEPTH, D_MODEL, 2 * D_FF), D_MODEL ** -0.5),
        "conv_w": nrm(ks[17], (DEPTH, CONV_WIDTH, D_FF), CONV_WIDTH ** -0.5),
        "conv_b": nrm(ks[18], (DEPTH, D_FF), 0.02),
        "w_down": nrm(ks[19], (DEPTH, D_FF, D_MODEL), D_FF ** -0.5),
        "g_ffn_post": 1.0 + nrm(ks[20], (DEPTH, D_MODEL), 0.1),
    }


def reference(x_prompt, x_sample, cache_a_k, cache_a_v, cache_b_k, cache_b_v, state_conv,
              g_mix_pre, w_in, sinks, rel_bias, w_oa, w_ob, w_out, g_mix_post,
              g_ffn_pre, w_up, conv_w, conv_b, w_down, g_ffn_post):
    yp = x_prompt
    ys = x_sample
    new_p = []
    new_s = []
    for l in range(DEPTH):
        w = (g_mix_pre[l], w_in[l], sinks[l], rel_bias[l], w_oa[l], w_ob[l], w_out[l],
             g_mix_post[l], g_ffn_pre[l], w_up[l], conv_w[l], conv_b[l], w_down[l], g_ffn_post[l])
        yp, sp = layer(yp, 0, None, None, None, None, None, *w)
        ys, ss = layer(ys, PAST_LEN, cache_a_k[l], cache_a_v[l], cache_b_k[l], cache_b_v[l],
                       state_conv[l], *w)
        new_p.append(sp)
        new_s.append(ss)

    def stack(lst, i):
        return jnp.stack([s[i] for s in lst], axis=0)

    return (yp, ys,
            stack(new_p, 0), stack(new_p, 1), stack(new_p, 2), stack(new_p, 3), stack(new_p, 4),
            stack(new_s, 0), stack(new_s, 1), stack(new_s, 2), stack(new_s, 3), stack(new_s, 4))
```

```python
import functools

import jax
import jax.numpy as jnp
from jax import lax
from jax.experimental import pallas as pl
from jax.experimental.pallas import tpu as pltpu

CHUNK = 64
HEAD_DIM = 64
A_Q_HEADS = 8
A_KV_HEADS = 2
A_WINDOW = 128
B_HEADS = 8
B_REACH = 512
REL_CLIP = 128
ROPE_THETA = 10000.0
CONV_WIDTH = 3
RMS_EPS = 1e-6
NEG_INF = -1e30
PAST_LEN = 1024

A_Q_W = A_Q_HEADS * HEAD_DIM
A_KV_W = A_KV_HEADS * HEAD_DIM
B_W = B_HEADS * HEAD_DIM
A_BAND = A_WINDOW + CHUNK
B_BAND = B_REACH + CHUNK
LANES = 128
N_PAIRS = A_Q_W // LANES
TOEPLITZ_W = 1024
FF_CHUNK = 256
VMEM_LIMIT = 56 * 1024 * 1024

BF16 = jnp.bfloat16
F32 = jnp.float32


def _resident(shape):
    nd = len(shape)
    return pl.BlockSpec(shape, lambda *_: (0,) * nd, pipeline_mode=pl.Buffered(1))


def _rms(x, g):
    return (x * lax.rsqrt(jnp.mean(x * x, axis=-1, keepdims=True) + RMS_EPS)) * g


def _bias_kernel(t_ref, o_ref):
    x = jnp.broadcast_to(t_ref[...], (CHUNK, TOEPLITZ_W))
    row = lax.broadcasted_iota(jnp.int32, (CHUNK, TOEPLITZ_W), 0)
    for b in range(CHUNK.bit_length() - 1):
        x = jnp.where((row >> b) & 1 == 1, pltpu.roll(x, 1 << b, axis=1), x)
    o_ref[...] = x[:, :B_BAND]


def _build_bias(rel_table):
    h = rel_table.shape[0]
    tab = rel_table.astype(F32)
    flat = B_REACH - REL_CLIP
    ramp = CHUNK + REL_CLIP
    row = jnp.concatenate([
        jnp.broadcast_to(tab[:, :1], (h, flat)),
        tab[:, :ramp],
        jnp.broadcast_to(tab[:, :1], (h, TOEPLITZ_W - flat - ramp)),
    ], axis=1).reshape(h, 1, TOEPLITZ_W)
    return pl.pallas_call(
        _bias_kernel,
        grid=(h,),
        in_specs=[pl.BlockSpec((None, 1, TOEPLITZ_W), lambda i: (i, 0, 0))],
        out_specs=pl.BlockSpec((None, CHUNK, B_BAND), lambda i: (i, 0, 0)),
        out_shape=jax.ShapeDtypeStruct((h, CHUNK, B_BAND), F32),
        name="rel_bias_tile",
    )(row)


def _rope(r, cos, sin_signed):
    lane = lax.broadcasted_iota(jnp.int32, (r.shape[0], LANES), 1)
    first_half = (lane & (HEAD_DIM // 2)) == 0
    out = []
    for j in range(r.shape[1] // LANES):
        blk = r[:, j * LANES:(j + 1) * LANES]
        up = pltpu.roll(blk, HEAD_DIM // 2, axis=1)
        down = pltpu.roll(blk, LANES - HEAD_DIM // 2, axis=1)
        out.append(blk * cos + jnp.where(first_half, down, up) * sin_signed)
    return out[0] if len(out) == 1 else jnp.concatenate(out, axis=1)


def _inproj_kernel(x_ref, g_ref, w_ref, cos_ref, sin_ref,
                   qa_ref, ka_ref, va_ref, qb_ref, kb_ref, vb_ref, sga_ref, sgb_ref,
                   ak_ref, av_ref, bk_ref, bv_ref, *, d_model, keep_a, keep_b, tm):
    i = pl.program_id(1)
    nt = pl.num_programs(1)
    h = _rms(x_ref[...], g_ref[...]).astype(BF16)
    scale = HEAD_DIM ** -0.5

    def proj(lo, width):
        return jnp.dot(h, w_ref[:, lo:lo + width], preferred_element_type=F32)

    def keep_tail(dst_ref, val, keep):
        if keep <= tm:
            @pl.when(i == nt - 1)
            def _():
                dst_ref[...] = val[tm - keep:, :]
        else:
            first = nt - keep // tm

            @pl.when(i >= first)
            def _():
                dst_ref[pl.ds(pl.multiple_of((i - first) * tm, tm), tm), :] = val

    cos = cos_ref[...]
    sin = sin_ref[...]
    off = 0
    qa_ref[...] = (_rope(proj(off, A_Q_W), cos, sin) * scale).astype(BF16)
    off += A_Q_W
    ka = _rope(proj(off, A_KV_W), cos, sin)
    ka_ref[...] = ka.astype(BF16)
    keep_tail(ak_ref, ka, keep_a)
    off += A_KV_W
    va = proj(off, A_KV_W)
    va_ref[...] = va.astype(BF16)
    keep_tail(av_ref, va, keep_a)
    off += A_KV_W
    qb_ref[...] = (proj(off, B_W) * scale).astype(BF16)
    off += B_W
    kb = proj(off, B_W)
    kb_ref[...] = kb.astype(BF16)
    keep_tail(bk_ref, kb, keep_b)
    off += B_W
    vb = proj(off, B_W)
    vb_ref[...] = vb.astype(BF16)
    keep_tail(bv_ref, vb, keep_b)
    off += B_W
    sga_ref[...] = jax.nn.sigmoid(proj(off, d_model)).astype(BF16)
    off += d_model
    sgb_ref[...] = jax.nn.sigmoid(proj(off, d_model)).astype(BF16)


def _in_projection(x2d, g, w_in_bf, cos, sin, *, n_seq, seq_rows, tm, keep_a, keep_b, name):
    rows, d_model = x2d.shape
    nt = seq_rows // tm
    tok = lambda width: pl.BlockSpec((tm, width), lambda b, i: (b * nt + i, 0))
    tail = lambda keep, width: pl.BlockSpec((None, keep, width), lambda b, i: (b, 0, 0))
    tok_shape = lambda width: jax.ShapeDtypeStruct((rows, width), BF16)
    tail_shape = lambda keep, width: jax.ShapeDtypeStruct((n_seq, keep, width), F32)
    return pl.pallas_call(
        functools.partial(_inproj_kernel, d_model=d_model, keep_a=keep_a, keep_b=keep_b, tm=tm),
        grid=(n_seq, nt),
        in_specs=[
            pl.BlockSpec((tm, d_model), lambda b, i: (b * nt + i, 0)),
            _resident((1, d_model)),
            _resident(w_in_bf.shape),
            pl.BlockSpec((tm, LANES), lambda b, i: (i, 0)),
            pl.BlockSpec((tm, LANES), lambda b, i: (i, 0)),
        ],
        out_specs=[tok(A_Q_W), tok(A_KV_W), tok(A_KV_W), tok(B_W), tok(B_W), tok(B_W),
                   tok(d_model), tok(d_model),
                   tail(keep_a, A_KV_W), tail(keep_a, A_KV_W), tail(keep_b, B_W), tail(keep_b, B_W)],
        out_shape=[tok_shape(A_Q_W), tok_shape(A_KV_W), tok_shape(A_KV_W), tok_shape(B_W), tok_shape(B_W),
                   tok_shape(B_W), tok_shape(d_model), tok_shape(d_model),
                   tail_shape(keep_a, A_KV_W), tail_shape(keep_a, A_KV_W),
                   tail_shape(keep_b, B_W), tail_shape(keep_b, B_W)],
        compiler_params=pltpu.CompilerParams(
            dimension_semantics=("arbitrary", "arbitrary"), vmem_limit_bytes=VMEM_LIMIT),
        name=name,
    )(x2d, g.reshape(1, d_model).astype(F32), w_in_bf, cos, sin)


def _pair_attention(q, k, v, *, bias=None, sink=None, valid=None):
    n = q.shape[0]
    lane = lax.broadcasted_iota(jnp.int32, q.shape, 1)
    lo = lane < HEAD_DIM
    zero = jnp.zeros_like(q)
    q2 = jnp.concatenate([jnp.where(lo, q, zero), jnp.where(lo, zero, q)], axis=0)
    s = lax.dot_general(q2, k, (((1,), (1,)), ((), ())), preferred_element_type=F32)
    if bias is not None:
        s = s + bias
    if valid is not None:
        s = jnp.where(valid, s, NEG_INF)
    m = jnp.max(s, axis=-1, keepdims=True)
    if sink is not None:
        m = jnp.maximum(m, sink)
    p = jnp.exp(s - m)
    denom = jnp.sum(p, axis=-1, keepdims=True)
    if sink is not None:
        denom = denom + jnp.exp(sink - m)
    o = jnp.dot(p.astype(BF16), v, preferred_element_type=F32) / denom
    return jnp.where(lo, o[:n], o[n:])


def _prompt_attn_kernel(qa_ref, ka_ref, va_ref, qb_ref, kb_ref, vb_ref, bias_ref, sink_ref,
                        oa_ref, ob_ref, kap, vap, kbp, vbp, *, seq_rows, tq):
    i = pl.program_id(1)

    @pl.when(i == 0)
    def _():
        kap[0:A_WINDOW, :] = jnp.zeros((A_WINDOW, LANES), BF16)
        vap[0:A_WINDOW, :] = jnp.zeros((A_WINDOW, LANES), BF16)
        kbp[0:B_REACH, :] = jnp.zeros((B_REACH, B_W), BF16)
        vbp[0:B_REACH, :] = jnp.zeros((B_REACH, B_W), BF16)

        def copy(t, carry):
            r = pl.multiple_of(t * tq, tq)
            ra = pl.multiple_of(A_WINDOW + r, A_WINDOW)
            rb = pl.multiple_of(B_REACH + r, B_REACH)
            kap[pl.ds(ra, tq), :] = ka_ref[pl.ds(r, tq), :]
            vap[pl.ds(ra, tq), :] = va_ref[pl.ds(r, tq), :]
            kbp[pl.ds(rb, tq), :] = kb_ref[pl.ds(r, tq), :]
            vbp[pl.ds(rb, tq), :] = vb_ref[pl.ds(r, tq), :]
            return carry

        lax.fori_loop(0, seq_rows // tq, copy, 0)

    col_a = lax.broadcasted_iota(jnp.int32, (2 * CHUNK, A_BAND), 1)
    col_b = lax.broadcasted_iota(jnp.int32, (2 * CHUNK, B_BAND), 1)

    def chunk(cc, carry):
        start = (i * (tq // CHUNK) + cc) * CHUNK
        r0 = pl.multiple_of(cc * CHUNK, CHUNK)
        w0 = pl.multiple_of(start, CHUNK)
        valid_a = col_a >= A_WINDOW - start
        valid_b = col_b >= B_REACH - start
        ka = kap[pl.ds(w0, A_BAND), :]
        va = vap[pl.ds(w0, A_BAND), :]
        for j in range(N_PAIRS):
            sl = slice(j * LANES, (j + 1) * LANES)
            o = _pair_attention(qa_ref[pl.ds(r0, CHUNK), sl], ka, va, sink=sink_ref[j], valid=valid_a)
            oa_ref[pl.ds(r0, CHUNK), sl] = o.astype(BF16)
        for j in range(N_PAIRS):
            sl = slice(j * LANES, (j + 1) * LANES)
            o = _pair_attention(qb_ref[pl.ds(r0, CHUNK), sl], kbp[pl.ds(w0, B_BAND), sl],
                                vbp[pl.ds(w0, B_BAND), sl], bias=bias_ref[j], valid=valid_b)
            ob_ref[pl.ds(r0, CHUNK), sl] = o.astype(BF16)
        return carry

    lax.fori_loop(0, tq // CHUNK, chunk, 0)


def _prompt_attention(qa, ka, va, qb, kb, vb, bias_pairs, sink_pairs, *, n_seq, seq_rows, tq):
    rows = n_seq * seq_rows
    nt = seq_rows // tq
    qspec = lambda width: pl.BlockSpec((tq, width), lambda b, i: (b * nt + i, 0))
    kvspec = lambda width: pl.BlockSpec((None, seq_rows, width), lambda b, i: (b, 0, 0))
    seq3 = lambda a: a.reshape(n_seq, seq_rows, a.shape[-1])
    return pl.pallas_call(
        functools.partial(_prompt_attn_kernel, seq_rows=seq_rows, tq=tq),
        grid=(n_seq, nt),
        in_specs=[qspec(A_Q_W), kvspec(A_KV_W), kvspec(A_KV_W), qspec(B_W), kvspec(B_W), kvspec(B_W),
                  _resident(bias_pairs.shape), _resident(sink_pairs.shape)],
        out_specs=[qspec(A_Q_W), qspec(B_W)],
        out_shape=[jax.ShapeDtypeStruct((rows, A_Q_W), BF16), jax.ShapeDtypeStruct((rows, B_W), BF16)],
        scratch_shapes=[pltpu.VMEM((A_WINDOW + seq_rows, A_KV_W), BF16),
                        pltpu.VMEM((A_WINDOW + seq_rows, A_KV_W), BF16),
                        pltpu.VMEM((B_REACH + seq_rows, B_W), BF16),
                        pltpu.VMEM((B_REACH + seq_rows, B_W), BF16)],
        compiler_params=pltpu.CompilerParams(
            dimension_semantics=("arbitrary", "arbitrary"), vmem_limit_bytes=VMEM_LIMIT),
        name="prompt_attention",
    )(qa, seq3(ka), seq3(va), qb, seq3(kb), seq3(vb), bias_pairs, sink_pairs)


def _sample_attn_kernel(qa_ref, ka_ref, va_ref, cak_ref, cav_ref, qb_ref, kb_ref, vb_ref, cbk_ref, cbv_ref,
                        bias_ref, sink_ref, oa_ref, ob_ref):
    ka = jnp.concatenate([cak_ref[...], ka_ref[...]], axis=0)
    va = jnp.concatenate([cav_ref[...], va_ref[...]], axis=0)
    for j in range(N_PAIRS):
        sl = slice(j * LANES, (j + 1) * LANES)
        oa_ref[:, sl] = _pair_attention(qa_ref[:, sl], ka, va, sink=sink_ref[j]).astype(BF16)
    for j in range(N_PAIRS):
        sl = slice(j * LANES, (j + 1) * LANES)
        kb = jnp.concatenate([cbk_ref[:, sl], kb_ref[:, sl]], axis=0)
        vb = jnp.concatenate([cbv_ref[:, sl], vb_ref[:, sl]], axis=0)
        ob_ref[:, sl] = _pair_attention(qb_ref[:, sl], kb, vb, bias=bias_ref[j]).astype(BF16)


def _sample_attention(qa, ka, va, cak, cav, qb, kb, vb, cbk, cbv, bias_pairs, sink_pairs, *, n_seq, seq_rows):
    rows = n_seq * seq_rows
    new = lambda width: pl.BlockSpec((seq_rows, width), lambda b: (b, 0))
    past = lambda a: pl.BlockSpec((None,) + a.shape[1:], lambda b: (b, 0, 0))
    return pl.pallas_call(
        _sample_attn_kernel,
        grid=(n_seq,),
        in_specs=[new(A_Q_W), new(A_KV_W), new(A_KV_W), past(cak), past(cav),
                  new(B_W), new(B_W), new(B_W), past(cbk), past(cbv),
                  _resident(bias_pairs.shape), _resident(sink_pairs.shape)],
        out_specs=[new(A_Q_W), new(B_W)],
        out_shape=[jax.ShapeDtypeStruct((rows, A_Q_W), BF16), jax.ShapeDtypeStruct((rows, B_W), BF16)],
        compiler_params=pltpu.CompilerParams(dimension_semantics=("arbitrary",)),
        name="sample_attention",
    )(qa, ka, va, cak, cav, qb, kb, vb, cbk, cbv, bias_pairs, sink_pairs)


def _post_kernel(*refs, tm, d_ff, seq_rows, carry_conv):
    if carry_conv:
        (x_ref, oa_ref, ob_ref, sga_ref, sgb_ref, woa_ref, wob_ref, wout_ref, gpost_ref, gpre_ref,
         wup_ref, cw_ref, cb_ref, wdown_ref, gffn_ref, y_ref, conv_ref, acc_ref, h2_ref) = refs
    else:
        (x_ref, oa_ref, ob_ref, sga_ref, sgb_ref, woa_ref, wob_ref, wout_ref, gpost_ref, gpre_ref,
         wup_ref, cw_ref, cb_ref, wdown_ref, gffn_ref, p1_ref, p2_ref, y_ref, u_ref, acc_ref, h2_ref) = refs
    i = pl.program_id(1)

    if carry_conv:
        @pl.when(i == 0)
        def _():
            conv_ref[...] = jnp.zeros_like(conv_ref)

    mixed = (sga_ref[...].astype(F32) * jnp.dot(oa_ref[...], woa_ref[...], preferred_element_type=F32)
             + sgb_ref[...].astype(F32) * jnp.dot(ob_ref[...], wob_ref[...], preferred_element_type=F32))
    z = jnp.dot(mixed.astype(BF16), wout_ref[...], preferred_element_type=F32)
    x1 = x_ref[...] + _rms(z, gpost_ref[...])
    y_ref[...] = x1
    h2_ref[...] = _rms(x1, gpre_ref[...]).astype(BF16)

    row = lax.broadcasted_iota(jnp.int32, (tm, FF_CHUNK), 0)
    row_in_seq = row % seq_rows
    sqrt_half = 0.7071067811865476
    for j in range(d_ff // FF_CHUNK):
        cols = slice(j * FF_CHUNK, (j + 1) * FF_CHUNK)
        h2 = h2_ref[...]
        u = jnp.dot(h2, wup_ref[:, cols], preferred_element_type=F32)
        val = jnp.dot(h2, wup_ref[:, d_ff + j * FF_CHUNK:d_ff + (j + 1) * FF_CHUNK],
                      preferred_element_type=F32)
        back1 = pltpu.roll(u, 1, axis=0)
        back2 = pltpu.roll(u, 2, axis=0)
        if carry_conv:
            c0 = conv_ref[0:1, cols]
            c1 = conv_ref[1:2, cols]
            back1 = jnp.where(row == 0, c1, back1)
            back2 = jnp.where(row == 0, c0, jnp.where(row == 1, c1, back2))
            conv_ref[:, cols] = u[tm - (CONV_WIDTH - 1):, :]
        else:
            back1 = jnp.where(row_in_seq == 0, p1_ref[:, cols], back1)
            back2 = jnp.where(row_in_seq < 2, p2_ref[:, cols], back2)
            u_ref[:, cols] = u
        conv = cb_ref[:, cols] + back2 * cw_ref[0:1, cols]
        conv = conv + back1 * cw_ref[1:2, cols]
        conv = conv + u * cw_ref[2:3, cols]
        act = 0.5 * conv * (1.0 + lax.erf(conv * sqrt_half))
        part = jnp.dot((act * val).astype(BF16), wdown_ref[cols, :], preferred_element_type=F32)
        if j == 0:
            acc_ref[...] = part
        else:
            acc_ref[...] += part
    y_ref[...] = y_ref[...] + _rms(acc_ref[...], gffn_ref[...])


def _post(x2d, oa, ob, sga, sgb, w, *, n_seq, seq_rows, tm, conv_bounds=None, name):
    rows, d_model = x2d.shape
    d_ff = w["w_down"].shape[0]
    carry_conv = conv_bounds is None
    nt = seq_rows // tm if carry_conv else 1
    grid = (n_seq, nt) if carry_conv else (1, 1)
    tok = lambda width: pl.BlockSpec((tm, width), lambda b, i: (b * nt + i, 0))
    vec = lambda a: a.reshape(1, -1).astype(F32)
    weights = [w["w_oa"], w["w_ob"], w["w_out"], vec(w["g_mix_post"]), vec(w["g_ffn_pre"]), w["w_up"],
               w["conv_w"].astype(F32), vec(w["conv_b"]), w["w_down"], vec(w["g_ffn_post"])]
    in_specs = [tok(d_model), tok(A_Q_W), tok(B_W), tok(d_model), tok(d_model)] + [_resident(a.shape) for a in weights]
    args = [x2d, oa, ob, sga, sgb] + weights
    if carry_conv:
        side_spec = pl.BlockSpec((None, CONV_WIDTH - 1, d_ff), lambda b, i: (b, 0, 0))
        side_shape = jax.ShapeDtypeStruct((n_seq, CONV_WIDTH - 1, d_ff), F32)
    else:
        in_specs += [tok(d_ff), tok(d_ff)]
        args += list(conv_bounds)
        side_spec = tok(d_ff)
        side_shape = jax.ShapeDtypeStruct((rows, d_ff), F32)
    return pl.pallas_call(
        functools.partial(_post_kernel, tm=tm, d_ff=d_ff, seq_rows=seq_rows, carry_conv=carry_conv),
        grid=grid,
        in_specs=in_specs,
        out_specs=[tok(d_model), side_spec],
        out_shape=[jax.ShapeDtypeStruct((rows, d_model), F32), side_shape],
        scratch_shapes=[pltpu.VMEM((tm, d_model), F32), pltpu.VMEM((tm, d_model), BF16)],
        compiler_params=pltpu.CompilerParams(
            dimension_semantics=("arbitrary", "arbitrary"), vmem_limit_bytes=VMEM_LIMIT),
        name=name,
    )(*args)


def _rope_tables(pos):
    half = HEAD_DIM // 2
    inv = 1.0 / (ROPE_THETA ** (jnp.arange(half, dtype=F32) * (2.0 / HEAD_DIM)))
    ang = pos.astype(F32)[:, None] * inv[None, :]
    cos, sin = jnp.cos(ang), jnp.sin(ang)
    reps = LANES // HEAD_DIM
    return (jnp.tile(jnp.concatenate([cos, cos], axis=1), (1, reps)),
            jnp.tile(jnp.concatenate([-sin, sin], axis=1), (1, reps)))


def _pair_heads_cols(w_q):
    d = w_q.shape[0]
    per = A_Q_HEADS // A_KV_HEADS
    return w_q.reshape(d, A_KV_HEADS, per, HEAD_DIM).transpose(0, 2, 1, 3).reshape(d, A_Q_W)


def kernel(x_prompt, x_sample, cache_a_k, cache_a_v, cache_b_k, cache_b_v, state_conv, g_mix_pre, w_in, sinks,
           rel_bias, w_oa, w_ob, w_out, g_mix_post, g_ffn_pre, w_up, conv_w, conv_b, w_down, g_ffn_post):
    depth = w_in.shape[0]
    n_p, t_p, d_model = x_prompt.shape
    n_s, t_s, _ = x_sample.shape
    d_ff = w_down.shape[1]
    per = A_Q_HEADS // A_KV_HEADS
    tm = 512

    cos_p, sin_p = _rope_tables(jnp.arange(t_p))
    cos_s, sin_s = _rope_tables(jnp.tile(PAST_LEN + jnp.arange(t_s), n_s))

    yp = x_prompt.reshape(n_p * t_p, d_model)
    ys = x_sample.reshape(n_s * t_s, d_model)
    new_p, new_s = [], []
    for l in range(depth):
        w_in_bf = jnp.concatenate([_pair_heads_cols(w_in[l][:, :A_Q_W]), w_in[l][:, A_Q_W:]], axis=1).astype(BF16)
        w = dict(
            w_oa=w_oa[l].reshape(A_KV_HEADS, per, HEAD_DIM, d_model).transpose(1, 0, 2, 3)
                        .reshape(A_Q_W, d_model).astype(BF16),
            w_ob=w_ob[l].astype(BF16), w_out=w_out[l].astype(BF16), g_mix_post=g_mix_post[l],
            g_ffn_pre=g_ffn_pre[l], w_up=w_up[l].astype(BF16), conv_w=conv_w[l], conv_b=conv_b[l],
            w_down=w_down[l].astype(BF16), g_ffn_post=g_ffn_post[l])
        sink_heads = sinks[l].astype(F32).reshape(A_KV_HEADS, per).T.reshape(N_PAIRS, 2)

        def sink_rows(n):
            return jnp.repeat(sink_heads, n, axis=1).reshape(N_PAIRS, 2 * n, 1)

        bias = _build_bias(rel_bias[l])
        bias_p = bias.reshape(N_PAIRS, 2 * CHUNK, B_BAND)
        keys_s = cache_b_k.shape[2] + t_s
        bias_s = bias[:, :t_s, :keys_s].reshape(N_PAIRS, 2 * t_s, keys_s)

        keep_a, keep_b = min(A_WINDOW, t_p), min(B_REACH, t_p)
        (qa, ka, va, qb, kb, vb, sga, sgb, ak, av, bk, bv) = _in_projection(
            yp, g_mix_pre[l], w_in_bf, cos_p, sin_p, n_seq=n_p, seq_rows=t_p, tm=tm,
            keep_a=keep_a, keep_b=keep_b, name="prompt_in_projection")
        oa, ob = _prompt_attention(qa, ka, va, qb, kb, vb, bias_p, sink_rows(CHUNK),
                                   n_seq=n_p, seq_rows=t_p, tq=tm)
        yp, conv_p = _post(yp, oa, ob, sga, sgb, w, n_seq=n_p, seq_rows=t_p, tm=tm, name="prompt_post")
        new_p.append((ak.reshape(n_p, keep_a, A_KV_HEADS, HEAD_DIM), av.reshape(n_p, keep_a, A_KV_HEADS, HEAD_DIM),
                      bk.reshape(n_p, keep_b, B_HEADS, HEAD_DIM), bv.reshape(n_p, keep_b, B_HEADS, HEAD_DIM),
                      conv_p))

        rows_s = n_s * t_s
        (qa, ka, va, qb, kb, vb, sga, sgb, ak, av, bk, bv) = _in_projection(
            ys, g_mix_pre[l], w_in_bf, cos_s, sin_s, n_seq=1, seq_rows=rows_s, tm=rows_s,
            keep_a=rows_s, keep_b=rows_s, name="sample_in_projection")
        flat_heads = lambda c: c.reshape(c.shape[0], c.shape[1], -1).astype(BF16)
        oa, ob = _sample_attention(qa, ka, va, flat_heads(cache_a_k[l]), flat_heads(cache_a_v[l]),
                                   qb, kb, vb, flat_heads(cache_b_k[l]), flat_heads(cache_b_v[l]),
                                   bias_s, sink_rows(t_s), n_seq=n_s, seq_rows=t_s)
        st = state_conv[l].astype(F32)
        zeros = jnp.zeros((n_s, t_s - 2, d_ff), F32)
        prev1 = jnp.concatenate([st[:, 1:2], zeros, zeros[:, :1]], axis=1).reshape(rows_s, d_ff)
        prev2 = jnp.concatenate([st, zeros], axis=1).reshape(rows_s, d_ff)
        ys, u_s = _post(ys, oa, ob, sga, sgb, w, n_seq=n_s, seq_rows=t_s, tm=rows_s,
                        conv_bounds=(prev1, prev2), name="sample_post")
        new_s.append((ak.reshape(n_s, t_s, A_KV_HEADS, HEAD_DIM), av.reshape(n_s, t_s, A_KV_HEADS, HEAD_DIM),
                      bk.reshape(n_s, t_s, B_HEADS, HEAD_DIM), bv.reshape(n_s, t_s, B_HEADS, HEAD_DIM),
                      u_s.reshape(n_s, t_s, d_ff)[:, t_s - (CONV_WIDTH - 1):]))

    stack = lambda lst, k: jnp.stack([s[k] for s in lst], axis=0)
    return (yp.reshape(n_p, t_p, d_model), ys.reshape(n_s, t_s, d_model),
            stack(new_p, 0), stack(new_p, 1), stack(new_p, 2), stack(new_p, 3), stack(new_p, 4),
            stack(new_s, 0), stack(new_s, 1), stack(new_s, 2), stack(new_s, 3), stack(new_s, 4))
```

```python
import functools

import jax
import jax.numpy as jnp
from jax import lax
from jax.experimental import pallas as pl
from jax.experimental.pallas import tpu as pltpu

CHUNK = 64
HEAD_DIM = 64
A_Q_HEADS = 8
A_KV_HEADS = 2
A_WINDOW = 128
B_HEADS = 8
B_REACH = 512
REL_CLIP = 128
ROPE_THETA = 10000.0
CONV_WIDTH = 3
RMS_EPS = 1e-6
NEG_INF = -1e30
PAST_LEN = 1024

A_Q_W = A_Q_HEADS * HEAD_DIM
A_KV_W = A_KV_HEADS * HEAD_DIM
B_W = B_HEADS * HEAD_DIM
A_BAND = A_WINDOW + CHUNK
B_BAND = B_REACH + CHUNK
LANES = 128
N_PAIRS = A_Q_W // LANES
TOEPLITZ_W = 1024
FF_CHUNK = 1024
VMEM_LIMIT = 56 * 1024 * 1024

BF16 = jnp.bfloat16
F32 = jnp.float32


def _resident(shape):
    nd = len(shape)
    return pl.BlockSpec(shape, lambda *_: (0,) * nd, pipeline_mode=pl.Buffered(1))


def _rms(x, g):
    return (x * lax.rsqrt(jnp.mean(x * x, axis=-1, keepdims=True) + RMS_EPS)) * g


def _bias_kernel(t_ref, o_ref):
    x = jnp.broadcast_to(t_ref[...], (CHUNK, TOEPLITZ_W))
    row = lax.broadcasted_iota(jnp.int32, (CHUNK, TOEPLITZ_W), 0)
    for b in range(CHUNK.bit_length() - 1):
        x = jnp.where((row >> b) & 1 == 1, pltpu.roll(x, 1 << b, axis=1), x)
    o_ref[...] = x[:, :B_BAND]


def _build_bias(rel_table):
    h = rel_table.shape[0]
    tab = rel_table.astype(F32)
    flat = B_REACH - REL_CLIP
    ramp = CHUNK + REL_CLIP
    row = jnp.concatenate([
        jnp.broadcast_to(tab[:, :1], (h, flat)),
        tab[:, :ramp],
        jnp.broadcast_to(tab[:, :1], (h, TOEPLITZ_W - flat - ramp)),
    ], axis=1).reshape(h, 1, TOEPLITZ_W)
    return pl.pallas_call(
        _bias_kernel,
        grid=(h,),
        in_specs=[pl.BlockSpec((None, 1, TOEPLITZ_W), lambda i: (i, 0, 0))],
        out_specs=pl.BlockSpec((None, CHUNK, B_BAND), lambda i: (i, 0, 0)),
        out_shape=jax.ShapeDtypeStruct((h, CHUNK, B_BAND), F32),
        name="rel_bias_tile",
    )(row)


def _rope(r, cos, sin_signed):
    lane = lax.broadcasted_iota(jnp.int32, (r.shape[0], LANES), 1)
    first_half = (lane & (HEAD_DIM // 2)) == 0
    out = []
    for j in range(r.shape[1] // LANES):
        blk = r[:, j * LANES:(j + 1) * LANES]
        up = pltpu.roll(blk, HEAD_DIM // 2, axis=1)
        down = pltpu.roll(blk, LANES - HEAD_DIM // 2, axis=1)
        out.append(blk * cos + jnp.where(first_half, down, up) * sin_signed)
    return out[0] if len(out) == 1 else jnp.concatenate(out, axis=1)


def _inproj_kernel(x_ref, g_ref, w_ref, cos_ref, sin_ref,
                   qa_ref, ka_ref, va_ref, qb_ref, kb_ref, vb_ref, sga_ref, sgb_ref,
                   ak_ref, av_ref, bk_ref, bv_ref, *, d_model, keep_a, keep_b, tm):
    i = pl.program_id(1)
    nt = pl.num_programs(1)
    h = _rms(x_ref[...], g_ref[...]).astype(BF16)
    scale = HEAD_DIM ** -0.5

    n_qkv = A_Q_W + 2 * A_KV_W + 3 * B_W
    qkv = jnp.dot(h, w_ref[:, :n_qkv], preferred_element_type=F32)
    gates = jnp.dot(h, w_ref[:, n_qkv:], preferred_element_type=F32)

    def proj(lo, width):
        return qkv[:, lo:lo + width] if lo < n_qkv else gates[:, lo - n_qkv:lo - n_qkv + width]

    def keep_tail(dst_ref, val, keep):
        if keep <= tm:
            @pl.when(i == nt - 1)
            def _():
                dst_ref[...] = val[tm - keep:, :]
        else:
            first = nt - keep // tm

            @pl.when(i >= first)
            def _():
                dst_ref[pl.ds(pl.multiple_of((i - first) * tm, tm), tm), :] = val

    cos = cos_ref[...]
    sin = sin_ref[...]
    off = 0
    qa_ref[...] = (_rope(proj(off, A_Q_W), cos, sin) * scale).astype(BF16)
    off += A_Q_W
    ka = _rope(proj(off, A_KV_W), cos, sin)
    ka_ref[...] = ka.astype(BF16)
    keep_tail(ak_ref, ka, keep_a)
    off += A_KV_W
    va = proj(off, A_KV_W)
    va_ref[...] = va.astype(BF16)
    keep_tail(av_ref, va, keep_a)
    off += A_KV_W
    qb_ref[...] = (proj(off, B_W) * scale).astype(BF16)
    off += B_W
    kb = proj(off, B_W)
    kb_ref[...] = kb.astype(BF16)
    keep_tail(bk_ref, kb, keep_b)
    off += B_W
    vb = proj(off, B_W)
    vb_ref[...] = vb.astype(BF16)
    keep_tail(bv_ref, vb, keep_b)
    off += B_W
    sga_ref[...] = jax.nn.sigmoid(proj(off, d_model)).astype(BF16)
    off += d_model
    sgb_ref[...] = jax.nn.sigmoid(proj(off, d_model)).astype(BF16)


def _in_projection(x2d, g, w_in_bf, cos, sin, *, n_seq, seq_rows, tm, keep_a, keep_b, name):
    rows, d_model = x2d.shape
    nt = seq_rows // tm
    tok = lambda width: pl.BlockSpec((tm, width), lambda b, i: (b * nt + i, 0))
    tail = lambda keep, width: pl.BlockSpec((None, keep, width), lambda b, i: (b, 0, 0))
    tok_shape = lambda width: jax.ShapeDtypeStruct((rows, width), BF16)
    tail_shape = lambda keep, width: jax.ShapeDtypeStruct((n_seq, keep, width), F32)
    return pl.pallas_call(
        functools.partial(_inproj_kernel, d_model=d_model, keep_a=keep_a, keep_b=keep_b, tm=tm),
        grid=(n_seq, nt),
        in_specs=[
            pl.BlockSpec((tm, d_model), lambda b, i: (b * nt + i, 0)),
            _resident((1, d_model)),
            _resident(w_in_bf.shape),
            pl.BlockSpec((tm, LANES), lambda b, i: (i, 0)),
            pl.BlockSpec((tm, LANES), lambda b, i: (i, 0)),
        ],
        out_specs=[tok(A_Q_W), tok(A_KV_W), tok(A_KV_W), tok(B_W), tok(B_W), tok(B_W),
                   tok(d_model), tok(d_model),
                   tail(keep_a, A_KV_W), tail(keep_a, A_KV_W), tail(keep_b, B_W), tail(keep_b, B_W)],
        out_shape=[tok_shape(A_Q_W), tok_shape(A_KV_W), tok_shape(A_KV_W), tok_shape(B_W), tok_shape(B_W),
                   tok_shape(B_W), tok_shape(d_model), tok_shape(d_model),
                   tail_shape(keep_a, A_KV_W), tail_shape(keep_a, A_KV_W),
                   tail_shape(keep_b, B_W), tail_shape(keep_b, B_W)],
        compiler_params=pltpu.CompilerParams(
            dimension_semantics=("arbitrary", "arbitrary"), vmem_limit_bytes=VMEM_LIMIT),
        name=name,
    )(x2d, g.reshape(1, d_model).astype(F32), w_in_bf, cos, sin)


def _stack_pair(q):
    lo = lax.broadcasted_iota(jnp.int32, q.shape, 1) < HEAD_DIM
    zero = jnp.zeros_like(q)
    return jnp.concatenate([jnp.where(lo, q, zero), jnp.where(lo, zero, q)], axis=0)


def _unstack_pair(o):
    n = o.shape[0] // 2
    lo = lax.broadcasted_iota(jnp.int32, (n, LANES), 1) < HEAD_DIM
    return jnp.where(lo, o[:n], o[n:])


def _scores(q2, k, *, bias=None, valid=None):
    s = lax.dot_general(q2, k, (((1,), (1,)), ((), ())), preferred_element_type=F32)
    if bias is not None:
        s = s + bias
    if valid is not None:
        s = jnp.where(valid, s, NEG_INF)
    return s


def _row_max(s, sink=None):
    m = jnp.max(s, axis=-1, keepdims=True)
    return m if sink is None else jnp.maximum(m, sink)


def _softmax_terms(s, m, *, sink=None):
    p = jnp.exp(s - m)
    denom = jnp.sum(p, axis=-1, keepdims=True)
    if sink is not None:
        denom = denom + jnp.exp(sink - m)
    return p.astype(BF16), denom


def _softmax_pv(s, m, v, *, sink=None):
    p, denom = _softmax_terms(s, m, sink=sink)
    return jnp.dot(p, v, preferred_element_type=F32) / denom


def _pair_attention(q, k, v, *, bias=None, sink=None):
    s = _scores(_stack_pair(q), k, bias=bias)
    return _unstack_pair(_softmax_pv(s, _row_max(s, sink), v, sink=sink))


def _prompt_attn_kernel(qa_ref, ka_ref, va_ref, qb_ref, kb_ref, vb_ref, bias_ref, sink_ref,
                        oa_ref, ob_ref, kap, vap, kbp, vbp, sa_scr, ma_scr, sb_scr, mb_scr, *, seq_rows, tq):
    i = pl.program_id(1)

    @pl.when(i == 0)
    def _():
        kap[0:A_WINDOW, :] = jnp.zeros((A_WINDOW, LANES), BF16)
        vap[0:A_WINDOW, :] = jnp.zeros((A_WINDOW, LANES), BF16)
        kbp[0:B_REACH, :] = jnp.zeros((B_REACH, B_W), BF16)
        vbp[0:B_REACH, :] = jnp.zeros((B_REACH, B_W), BF16)

        def copy(t, carry):
            r = pl.multiple_of(t * tq, tq)
            ra = pl.multiple_of(A_WINDOW + r, A_WINDOW)
            rb = pl.multiple_of(B_REACH + r, B_REACH)
            kap[pl.ds(ra, tq), :] = ka_ref[pl.ds(r, tq), :]
            vap[pl.ds(ra, tq), :] = va_ref[pl.ds(r, tq), :]
            kbp[pl.ds(rb, tq), :] = kb_ref[pl.ds(r, tq), :]
            vbp[pl.ds(rb, tq), :] = vb_ref[pl.ds(r, tq), :]
            return carry

        lax.fori_loop(0, seq_rows // tq, copy, 0)

    col_a = lax.broadcasted_iota(jnp.int32, (2 * CHUNK, A_BAND), 1)
    col_b = lax.broadcasted_iota(jnp.int32, (2 * CHUNK, B_BAND), 1)

    def chunk(cc, masked):
        start = (i * (tq // CHUNK) + cc) * CHUNK
        r0 = pl.multiple_of(cc * CHUNK, CHUNK)
        w0 = pl.multiple_of(start, CHUNK)
        valid_a = (col_a >= A_WINDOW - start) if masked else None
        valid_b = (col_b >= B_REACH - start) if masked else None
        for j in range(N_PAIRS):
            sl = slice(j * LANES, (j + 1) * LANES)
            s = _scores(_stack_pair(qa_ref[pl.ds(r0, CHUNK), sl]), kap[pl.ds(w0, A_BAND), :], valid=valid_a)
            sa_scr[j] = s
            ma_scr[j] = _row_max(s, sink_ref[j])
        for j in range(N_PAIRS):
            sl = slice(j * LANES, (j + 1) * LANES)
            s = _scores(_stack_pair(qb_ref[pl.ds(r0, CHUNK), sl]), kbp[pl.ds(w0, B_BAND), sl],
                        bias=bias_ref[j], valid=valid_b)
            sb_scr[j] = s
            mb_scr[j] = _row_max(s)
        for j in range(N_PAIRS):
            sl = slice(j * LANES, (j + 1) * LANES)
            o = _softmax_pv(sa_scr[j], ma_scr[j], vap[pl.ds(w0, A_BAND), :], sink=sink_ref[j])
            oa_ref[pl.ds(r0, CHUNK), sl] = _unstack_pair(o).astype(BF16)
        for j in range(N_PAIRS):
            sl = slice(j * LANES, (j + 1) * LANES)
            o = _softmax_pv(sb_scr[j], mb_scr[j], vbp[pl.ds(w0, B_BAND), sl])
            ob_ref[pl.ds(r0, CHUNK), sl] = _unstack_pair(o).astype(BF16)

    def run(masked):
        def one_chunk(cc, carry):
            chunk(cc, masked)
            return carry

        lax.fori_loop(0, tq // CHUNK, one_chunk, 0)

    @pl.when(i == 0)
    def _():
        run(True)

    @pl.when(i != 0)
    def _():
        run(False)


def _prompt_attention(qa, ka, va, qb, kb, vb, bias_pairs, sink_pairs, *, n_seq, seq_rows, tq):
    rows = n_seq * seq_rows
    nt = seq_rows // tq
    qspec = lambda width: pl.BlockSpec((tq, width), lambda b, i: (b * nt + i, 0))
    kvspec = lambda width: pl.BlockSpec((None, seq_rows, width), lambda b, i: (b, 0, 0))
    seq3 = lambda a: a.reshape(n_seq, seq_rows, a.shape[-1])
    return pl.pallas_call(
        functools.partial(_prompt_attn_kernel, seq_rows=seq_rows, tq=tq),
        grid=(n_seq, nt),
        in_specs=[qspec(A_Q_W), kvspec(A_KV_W), kvspec(A_KV_W), qspec(B_W), kvspec(B_W), kvspec(B_W),
                  _resident(bias_pairs.shape), _resident(sink_pairs.shape)],
        out_specs=[qspec(A_Q_W), qspec(B_W)],
        out_shape=[jax.ShapeDtypeStruct((rows, A_Q_W), BF16), jax.ShapeDtypeStruct((rows, B_W), BF16)],
        scratch_shapes=[pltpu.VMEM((A_WINDOW + seq_rows, A_KV_W), BF16),
                        pltpu.VMEM((A_WINDOW + seq_rows, A_KV_W), BF16),
                        pltpu.VMEM((B_REACH + seq_rows, B_W), BF16),
                        pltpu.VMEM((B_REACH + seq_rows, B_W), BF16),
                        pltpu.VMEM((N_PAIRS, 2 * CHUNK, A_BAND), F32),
                        pltpu.VMEM((N_PAIRS, 2 * CHUNK, 1), F32),
                        pltpu.VMEM((N_PAIRS, 2 * CHUNK, B_BAND), F32),
                        pltpu.VMEM((N_PAIRS, 2 * CHUNK, 1), F32)],
        compiler_params=pltpu.CompilerParams(
            dimension_semantics=("arbitrary", "arbitrary"), vmem_limit_bytes=VMEM_LIMIT),
        name="prompt_attention",
    )(qa, seq3(ka), seq3(va), qb, seq3(kb), seq3(vb), bias_pairs, sink_pairs)


def _sample_attn_kernel(qa_ref, ka_ref, va_ref, cak_ref, cav_ref, qb_ref, kb_ref, vb_ref, cbk_ref, cbv_ref,
                        bias_ref, sink_ref, oa_ref, ob_ref):
    ka = jnp.concatenate([cak_ref[...], ka_ref[...]], axis=0)
    va = jnp.concatenate([cav_ref[...], va_ref[...]], axis=0)
    for j in range(N_PAIRS):
        sl = slice(j * LANES, (j + 1) * LANES)
        oa_ref[:, sl] = _pair_attention(qa_ref[:, sl], ka, va, sink=sink_ref[j]).astype(BF16)
    for j in range(N_PAIRS):
        sl = slice(j * LANES, (j + 1) * LANES)
        kb = jnp.concatenate([cbk_ref[:, sl], kb_ref[:, sl]], axis=0)
        vb = jnp.concatenate([cbv_ref[:, sl], vb_ref[:, sl]], axis=0)
        ob_ref[:, sl] = _pair_attention(qb_ref[:, sl], kb, vb, bias=bias_ref[j]).astype(BF16)


def _sample_attention(qa, ka, va, cak, cav, qb, kb, vb, cbk, cbv, bias_pairs, sink_pairs, *, n_seq, seq_rows):
    rows = n_seq * seq_rows
    new = lambda width: pl.BlockSpec((seq_rows, width), lambda b: (b, 0))
    past = lambda a: pl.BlockSpec((None,) + a.shape[1:], lambda b: (b, 0, 0))
    return pl.pallas_call(
        _sample_attn_kernel,
        grid=(n_seq,),
        in_specs=[new(A_Q_W), new(A_KV_W), new(A_KV_W), past(cak), past(cav),
                  new(B_W), new(B_W), new(B_W), past(cbk), past(cbv),
                  _resident(bias_pairs.shape), _resident(sink_pairs.shape)],
        out_specs=[new(A_Q_W), new(B_W)],
        out_shape=[jax.ShapeDtypeStruct((rows, A_Q_W), BF16), jax.ShapeDtypeStruct((rows, B_W), BF16)],
        compiler_params=pltpu.CompilerParams(dimension_semantics=("arbitrary",)),
        name="sample_attention",
    )(qa, ka, va, cak, cav, qb, kb, vb, cbk, cbv, bias_pairs, sink_pairs)


def _ff_chunks(d_ff):
    return [(lo, min(FF_CHUNK, d_ff - lo)) for lo in range(0, d_ff, FF_CHUNK)]


def _interleave_up(w_up, d_ff):
    return jnp.concatenate([w_up[:, base + lo:base + lo + width]
                            for lo, width in _ff_chunks(d_ff) for base in (0, d_ff)], axis=1)


def _post_kernel(*refs, tm, d_ff, seq_rows, carry_conv):
    if carry_conv:
        (x_ref, oa_ref, ob_ref, sga_ref, sgb_ref, woa_ref, wob_ref, wout_ref, gpost_ref, gpre_ref,
         wup_ref, cw_ref, cb_ref, wdown_ref, gffn_ref, y_ref, conv_ref, h2_ref) = refs
    else:
        (x_ref, oa_ref, ob_ref, sga_ref, sgb_ref, woa_ref, wob_ref, wout_ref, gpost_ref, gpre_ref,
         wup_ref, cw_ref, cb_ref, wdown_ref, gffn_ref, p1_ref, p2_ref, y_ref, u_ref, h2_ref) = refs
    i = pl.program_id(1)

    if carry_conv:
        @pl.when(i == 0)
        def _():
            conv_ref[...] = jnp.zeros_like(conv_ref)

    mixed = (sga_ref[...].astype(F32) * jnp.dot(oa_ref[...], woa_ref[...], preferred_element_type=F32)
             + sgb_ref[...].astype(F32) * jnp.dot(ob_ref[...], wob_ref[...], preferred_element_type=F32))
    z = jnp.dot(mixed.astype(BF16), wout_ref[...], preferred_element_type=F32)
    x1 = x_ref[...] + _rms(z, gpost_ref[...])
    y_ref[...] = x1
    h2_ref[...] = _rms(x1, gpre_ref[...]).astype(BF16)

    sqrt_half = 0.7071067811865476
    ff = None
    for lo, width in _ff_chunks(d_ff):
        cols = slice(lo, lo + width)
        uv = jnp.dot(h2_ref[...], wup_ref[:, 2 * lo:2 * (lo + width)], preferred_element_type=F32)
        u = uv[:, :width]
        val = uv[:, width:]
        row = lax.broadcasted_iota(jnp.int32, (tm, width), 0)
        back1 = pltpu.roll(u, 1, axis=0)
        back2 = pltpu.roll(u, 2, axis=0)
        if carry_conv:
            c0 = conv_ref[0:1, cols]
            c1 = conv_ref[1:2, cols]
            back1 = jnp.where(row == 0, c1, back1)
            back2 = jnp.where(row == 0, c0, jnp.where(row == 1, c1, back2))
            conv_ref[:, cols] = u[tm - (CONV_WIDTH - 1):, :]
        else:
            row_in_seq = row % seq_rows
            back1 = jnp.where(row_in_seq == 0, p1_ref[:, cols], back1)
            back2 = jnp.where(row_in_seq < 2, p2_ref[:, cols], back2)
            u_ref[:, cols] = u
        conv = cb_ref[:, cols] + back2 * cw_ref[0:1, cols]
        conv = conv + back1 * cw_ref[1:2, cols]
        conv = conv + u * cw_ref[2:3, cols]
        act = 0.5 * conv * (1.0 + lax.erf(conv * sqrt_half))
        part = jnp.dot((act * val).astype(BF16), wdown_ref[cols, :], preferred_element_type=F32)
        ff = part if ff is None else ff + part
    y_ref[...] = y_ref[...] + _rms(ff, gffn_ref[...])


def _post(x2d, oa, ob, sga, sgb, w, *, n_seq, seq_rows, tm, conv_bounds=None, name):
    rows, d_model = x2d.shape
    d_ff = w["w_down"].shape[0]
    carry_conv = conv_bounds is None
    nt = seq_rows // tm if carry_conv else 1
    grid = (n_seq, nt) if carry_conv else (1, 1)
    tok = lambda width: pl.BlockSpec((tm, width), lambda b, i: (b * nt + i, 0))
    vec = lambda a: a.reshape(1, -1).astype(F32)
    weights = [w["w_oa"], w["w_ob"], w["w_out"], vec(w["g_mix_post"]), vec(w["g_ffn_pre"]), w["w_up"],
               w["conv_w"].astype(F32), vec(w["conv_b"]), w["w_down"], vec(w["g_ffn_post"])]
    in_specs = [tok(d_model), tok(A_Q_W), tok(B_W), tok(d_model), tok(d_model)] + [_resident(a.shape) for a in weights]
    args = [x2d, oa, ob, sga, sgb] + weights
    if carry_conv:
        side_spec = pl.BlockSpec((None, CONV_WIDTH - 1, d_ff), lambda b, i: (b, 0, 0))
        side_shape = jax.ShapeDtypeStruct((n_seq, CONV_WIDTH - 1, d_ff), F32)
    else:
        in_specs += [tok(d_ff), tok(d_ff)]
        args += list(conv_bounds)
        side_spec = tok(d_ff)
        side_shape = jax.ShapeDtypeStruct((rows, d_ff), F32)
    return pl.pallas_call(
        functools.partial(_post_kernel, tm=tm, d_ff=d_ff, seq_rows=seq_rows, carry_conv=carry_conv),
        grid=grid,
        in_specs=in_specs,
        out_specs=[tok(d_model), side_spec],
        out_shape=[jax.ShapeDtypeStruct((rows, d_model), F32), side_shape],
        scratch_shapes=[pltpu.VMEM((tm, d_model), BF16)],
        compiler_params=pltpu.CompilerParams(
            dimension_semantics=("arbitrary", "arbitrary"), vmem_limit_bytes=VMEM_LIMIT),
        name=name,
    )(*args)


def _rope_tables(pos):
    half = HEAD_DIM // 2
    inv = 1.0 / (ROPE_THETA ** (jnp.arange(half, dtype=F32) * (2.0 / HEAD_DIM)))
    ang = pos.astype(F32)[:, None] * inv[None, :]
    cos, sin = jnp.cos(ang), jnp.sin(ang)
    reps = LANES // HEAD_DIM
    return (jnp.tile(jnp.concatenate([cos, cos], axis=1), (1, reps)),
            jnp.tile(jnp.concatenate([-sin, sin], axis=1), (1, reps)))


def _pair_heads_cols(w_q):
    d = w_q.shape[0]
    per = A_Q_HEADS // A_KV_HEADS
    return w_q.reshape(d, A_KV_HEADS, per, HEAD_DIM).transpose(0, 2, 1, 3).reshape(d, A_Q_W)


def kernel(x_prompt, x_sample, cache_a_k, cache_a_v, cache_b_k, cache_b_v, state_conv, g_mix_pre, w_in, sinks,
           rel_bias, w_oa, w_ob, w_out, g_mix_post, g_ffn_pre, w_up, conv_w, conv_b, w_down, g_ffn_post):
    depth = w_in.shape[0]
    n_p, t_p, d_model = x_prompt.shape
    n_s, t_s, _ = x_sample.shape
    d_ff = w_down.shape[1]
    per = A_Q_HEADS // A_KV_HEADS
    tm = 512

    cos_p, sin_p = _rope_tables(jnp.arange(t_p))
    cos_s, sin_s = _rope_tables(jnp.tile(PAST_LEN + jnp.arange(t_s), n_s))

    yp = x_prompt.reshape(n_p * t_p, d_model)
    ys = x_sample.reshape(n_s * t_s, d_model)
    new_p, new_s = [], []
    for l in range(depth):
        w_in_bf = jnp.concatenate([_pair_heads_cols(w_in[l][:, :A_Q_W]), w_in[l][:, A_Q_W:]], axis=1).astype(BF16)
        w = dict(
            w_oa=w_oa[l].reshape(A_KV_HEADS, per, HEAD_DIM, d_model).transpose(1, 0, 2, 3)
                        .reshape(A_Q_W, d_model).astype(BF16),
            w_ob=w_ob[l].astype(BF16), w_out=w_out[l].astype(BF16), g_mix_post=g_mix_post[l],
            g_ffn_pre=g_ffn_pre[l], w_up=_interleave_up(w_up[l], d_ff).astype(BF16), conv_w=conv_w[l],
            conv_b=conv_b[l],
            w_down=w_down[l].astype(BF16), g_ffn_post=g_ffn_post[l])
        sink_heads = sinks[l].astype(F32).reshape(A_KV_HEADS, per).T.reshape(N_PAIRS, 2)

        def sink_rows(n):
            return jnp.repeat(sink_heads, n, axis=1).reshape(N_PAIRS, 2 * n, 1)

        bias = _build_bias(rel_bias[l])
        bias_p = bias.reshape(N_PAIRS, 2 * CHUNK, B_BAND)
        keys_s = cache_b_k.shape[2] + t_s
        bias_s = bias[:, :t_s, :keys_s].reshape(N_PAIRS, 2 * t_s, keys_s)

        keep_a, keep_b = min(A_WINDOW, t_p), min(B_REACH, t_p)
        (qa, ka, va, qb, kb, vb, sga, sgb, ak, av, bk, bv) = _in_projection(
            yp, g_mix_pre[l], w_in_bf, cos_p, sin_p, n_seq=n_p, seq_rows=t_p, tm=tm,
            keep_a=keep_a, keep_b=keep_b, name="prompt_in_projection")
        oa, ob = _prompt_attention(qa, ka, va, qb, kb, vb, bias_p, sink_rows(CHUNK),
                                   n_seq=n_p, seq_rows=t_p, tq=tm)
        yp, conv_p = _post(yp, oa, ob, sga, sgb, w, n_seq=n_p, seq_rows=t_p, tm=tm, name="prompt_post")
        new_p.append((ak.reshape(n_p, keep_a, A_KV_HEADS, HEAD_DIM), av.reshape(n_p, keep_a, A_KV_HEADS, HEAD_DIM),
                      bk.reshape(n_p, keep_b, B_HEADS, HEAD_DIM), bv.reshape(n_p, keep_b, B_HEADS, HEAD_DIM),
                      conv_p))

        rows_s = n_s * t_s
        (qa, ka, va, qb, kb, vb, sga, sgb, ak, av, bk, bv) = _in_projection(
            ys, g_mix_pre[l], w_in_bf, cos_s, sin_s, n_seq=1, seq_rows=rows_s, tm=rows_s,
            keep_a=rows_s, keep_b=rows_s, name="sample_in_projection")
        flat_heads = lambda c: c.reshape(c.shape[0], c.shape[1], -1).astype(BF16)
        oa, ob = _sample_attention(qa, ka, va, flat_heads(cache_a_k[l]), flat_heads(cache_a_v[l]),
                                   qb, kb, vb, flat_heads(cache_b_k[l]), flat_heads(cache_b_v[l]),
                                   bias_s, sink_rows(t_s), n_seq=n_s, seq_rows=t_s)
        st = state_conv[l].astype(F32)
        zeros = jnp.zeros((n_s, t_s - 2, d_ff), F32)
        prev1 = jnp.concatenate([st[:, 1:2], zeros, zeros[:, :1]], axis=1).reshape(rows_s, d_ff)
        prev2 = jnp.concatenate([st, zeros], axis=1).reshape(rows_s, d_ff)
        ys, u_s = _post(ys, oa, ob, sga, sgb, w, n_seq=n_s, seq_rows=t_s, tm=rows_s,
                        conv_bounds=(prev1, prev2), name="sample_post")
        new_s.append((ak.reshape(n_s, t_s, A_KV_HEADS, HEAD_DIM), av.reshape(n_s, t_s, A_KV_HEADS, HEAD_DIM),
                      bk.reshape(n_s, t_s, B_HEADS, HEAD_DIM), bv.reshape(n_s, t_s, B_HEADS, HEAD_DIM),
                      u_s.reshape(n_s, t_s, d_ff)[:, t_s - (CONV_WIDTH - 1):]))

    stack = lambda lst, k: jnp.stack([s[k] for s in lst], axis=0)
    return (yp.reshape(n_p, t_p, d_model), ys.reshape(n_s, t_s, d_model),
            stack(new_p, 0), stack(new_p, 1), stack(new_p, 2), stack(new_p, 3), stack(new_p, 4),
            stack(new_s, 0), stack(new_s, 1), stack(new_s, 2), stack(new_s, 3), stack(new_s, 4))
```

```python
import functools

import jax
import jax.numpy as jnp
from jax import lax
from jax.experimental import pallas as pl
from jax.experimental.pallas import tpu as pltpu

CHUNK = 64
HEAD_DIM = 64
A_Q_HEADS = 8
A_KV_HEADS = 2
A_WINDOW = 128
B_HEADS = 8
B_REACH = 512
REL_CLIP = 128
ROPE_THETA = 10000.0
CONV_WIDTH = 3
RMS_EPS = 1e-6
NEG_INF = -1e30
PAST_LEN = 1024

A_Q_W = A_Q_HEADS * HEAD_DIM
A_KV_W = A_KV_HEADS * HEAD_DIM
B_W = B_HEADS * HEAD_DIM
A_BAND = A_WINDOW + CHUNK
B_BAND = B_REACH + CHUNK
B_BAND_PAD = 640
LOG2E = 1.4426950408889634
LANES = 128
N_PAIRS = A_Q_W // LANES
TOEPLITZ_W = 1024
PART_ROWS = 256
FF_CHUNK = 1024
VMEM_LIMIT = 56 * 1024 * 1024

BF16 = jnp.bfloat16
F32 = jnp.float32


def _resident(shape):
    nd = len(shape)
    return pl.BlockSpec(shape, lambda *_: (0,) * nd, pipeline_mode=pl.Buffered(1))


def _rms(x, g):
    return (x * lax.rsqrt(jnp.mean(x * x, axis=-1, keepdims=True) + RMS_EPS)) * g


def _bias_kernel(t_ref, o_ref):
    x = jnp.broadcast_to(t_ref[...], (CHUNK, TOEPLITZ_W))
    row = lax.broadcasted_iota(jnp.int32, (CHUNK, TOEPLITZ_W), 0)
    for b in range(CHUNK.bit_length() - 1):
        x = jnp.where((row >> b) & 1 == 1, pltpu.roll(x, 1 << b, axis=1), x)
    col = lax.broadcasted_iota(jnp.int32, (CHUNK, B_BAND_PAD), 1)
    o_ref[...] = jnp.where(col < B_BAND, x[:, :B_BAND_PAD], NEG_INF)


def _build_bias(rel_table):
    h = rel_table.shape[0]
    tab = rel_table.astype(F32) * LOG2E
    flat = B_REACH - REL_CLIP
    ramp = CHUNK + REL_CLIP
    row = jnp.concatenate([
        jnp.broadcast_to(tab[:, :1], (h, flat)),
        tab[:, :ramp],
        jnp.broadcast_to(tab[:, :1], (h, TOEPLITZ_W - flat - ramp)),
    ], axis=1).reshape(h, 1, TOEPLITZ_W)
    return pl.pallas_call(
        _bias_kernel,
        grid=(h,),
        in_specs=[pl.BlockSpec((None, 1, TOEPLITZ_W), lambda i: (i, 0, 0))],
        out_specs=pl.BlockSpec((None, CHUNK, B_BAND_PAD), lambda i: (i, 0, 0)),
        out_shape=jax.ShapeDtypeStruct((h, CHUNK, B_BAND_PAD), F32),
        name="rel_bias_tile",
    )(row)


def _rope(r, cos, sin_signed):
    lane = lax.broadcasted_iota(jnp.int32, (r.shape[0], LANES), 1)
    first_half = (lane & (HEAD_DIM // 2)) == 0
    out = []
    for j in range(r.shape[1] // LANES):
        blk = r[:, j * LANES:(j + 1) * LANES]
        up = pltpu.roll(blk, HEAD_DIM // 2, axis=1)
        down = pltpu.roll(blk, LANES - HEAD_DIM // 2, axis=1)
        out.append(blk * cos + jnp.where(first_half, down, up) * sin_signed)
    return out[0] if len(out) == 1 else jnp.concatenate(out, axis=1)


def _inproj_kernel(x_ref, g_ref, w_ref, cos_ref, sin_ref,
                   qa_ref, ka_ref, va_ref, qb_ref, kb_ref, vb_ref, sga_ref, sgb_ref,
                   ak_ref, av_ref, bk_ref, bv_ref, *, d_model, keep_a, keep_b, tm, n_split):
    i = pl.program_id(1)
    nt = pl.num_programs(1)
    scale = HEAD_DIM ** -0.5 * LOG2E
    part = tm // n_split

    def keep_tail(dst_ref, val, keep, r0):
        if keep <= tm:
            lo = max(r0, tm - keep)
            if lo < r0 + part:
                dst_ref[lo - (tm - keep):r0 + part - (tm - keep), :] = val[lo - r0:, :]
        else:
            first = nt - keep // tm

            @pl.when(i >= first)
            def _():
                dst_ref[pl.ds(pl.multiple_of((i - first) * tm + r0, part), part), :] = val

    for r0 in range(0, tm, part):
        rows = slice(r0, r0 + part)
        h = _rms(x_ref[rows, :], g_ref[...]).astype(BF16)
        r = jnp.dot(h, w_ref[...], preferred_element_type=F32)
        cos = cos_ref[rows, :]
        sin = sin_ref[rows, :]
        off = 0
        qa_ref[rows, :] = (_rope(r[:, off:off + A_Q_W], cos, sin) * scale).astype(BF16)
        off += A_Q_W
        ka = _rope(r[:, off:off + A_KV_W], cos, sin)
        ka_ref[rows, :] = ka.astype(BF16)
        keep_tail(ak_ref, ka, keep_a, r0)
        off += A_KV_W
        va = r[:, off:off + A_KV_W]
        va_ref[rows, :] = va.astype(BF16)
        keep_tail(av_ref, va, keep_a, r0)
        off += A_KV_W
        qb_ref[rows, :] = (r[:, off:off + B_W] * scale).astype(BF16)
        off += B_W
        kb = r[:, off:off + B_W]
        kb_ref[rows, :] = kb.astype(BF16)
        keep_tail(bk_ref, kb, keep_b, r0)
        off += B_W
        vb = r[:, off:off + B_W]
        vb_ref[rows, :] = vb.astype(BF16)
        keep_tail(bv_ref, vb, keep_b, r0)
        off += B_W
        sga_ref[rows, :] = jax.nn.sigmoid(r[:, off:off + d_model]).astype(BF16)
        off += d_model
        sgb_ref[rows, :] = jax.nn.sigmoid(r[:, off:off + d_model]).astype(BF16)


def _in_projection(x2d, g, w_in_bf, cos, sin, *, n_seq, seq_rows, tm, keep_a, keep_b, name):
    rows, d_model = x2d.shape
    nt = seq_rows // tm
    tok = lambda width: pl.BlockSpec((tm, width), lambda b, i: (b * nt + i, 0))
    tail = lambda keep, width: pl.BlockSpec((None, keep, width), lambda b, i: (b, 0, 0))
    tok_shape = lambda width: jax.ShapeDtypeStruct((rows, width), BF16)
    tail_shape = lambda keep, width: jax.ShapeDtypeStruct((n_seq, keep, width), F32)
    return pl.pallas_call(
        functools.partial(_inproj_kernel, d_model=d_model, keep_a=keep_a, keep_b=keep_b, tm=tm,
                          n_split=max(1, tm // PART_ROWS)),
        grid=(n_seq, nt),
        in_specs=[
            pl.BlockSpec((tm, d_model), lambda b, i: (b * nt + i, 0)),
            _resident((1, d_model)),
            _resident(w_in_bf.shape),
            pl.BlockSpec((tm, LANES), lambda b, i: (i, 0)),
            pl.BlockSpec((tm, LANES), lambda b, i: (i, 0)),
        ],
        out_specs=[tok(A_Q_W), tok(A_KV_W), tok(A_KV_W), tok(B_W), tok(B_W), tok(B_W),
                   tok(d_model), tok(d_model),
                   tail(keep_a, A_KV_W), tail(keep_a, A_KV_W), tail(keep_b, B_W), tail(keep_b, B_W)],
        out_shape=[tok_shape(A_Q_W), tok_shape(A_KV_W), tok_shape(A_KV_W), tok_shape(B_W), tok_shape(B_W),
                   tok_shape(B_W), tok_shape(d_model), tok_shape(d_model),
                   tail_shape(keep_a, A_KV_W), tail_shape(keep_a, A_KV_W),
                   tail_shape(keep_b, B_W), tail_shape(keep_b, B_W)],
        compiler_params=pltpu.CompilerParams(
            dimension_semantics=("arbitrary", "arbitrary"), vmem_limit_bytes=VMEM_LIMIT),
        name=name,
    )(x2d, g.reshape(1, d_model).astype(F32), w_in_bf, cos, sin)


def _stack_pair(q):
    lo = lax.broadcasted_iota(jnp.int32, q.shape, 1) < HEAD_DIM
    zero = jnp.zeros_like(q)
    return jnp.concatenate([jnp.where(lo, q, zero), jnp.where(lo, zero, q)], axis=0)


def _unstack_pair(o):
    n = o.shape[0] // 2
    lo = lax.broadcasted_iota(jnp.int32, (n, LANES), 1) < HEAD_DIM
    return jnp.where(lo, o[:n], o[n:])


def _scores(q2, k, *, bias=None, valid=None):
    s = lax.dot_general(q2, k, (((1,), (1,)), ((), ())), preferred_element_type=F32)
    if bias is not None:
        s = s + bias
    if valid is not None:
        s = jnp.where(valid, s, NEG_INF)
    return s


def _row_max(s, sink=None):
    m = jnp.broadcast_to(jnp.max(s, axis=-1, keepdims=True), (s.shape[0], LANES))
    return m if sink is None else jnp.maximum(m, sink)


def _softmax_pv(s, m, v, *, sink=None):
    nk = s.shape[1]
    p = jnp.concatenate([jnp.exp2(s[:, k:min(k + LANES, nk)] - m[:, :min(LANES, nk - k)])
                         for k in range(0, nk, LANES)], axis=1)
    denom = jnp.broadcast_to(jnp.sum(p, axis=-1, keepdims=True), m.shape)
    if sink is not None:
        denom = denom + jnp.exp2(sink - m)
    return jnp.dot(p.astype(BF16), v, preferred_element_type=F32) * (1.0 / denom)


def _pair_attention(q, k, v, *, bias=None, sink=None):
    s = _scores(_stack_pair(q), k, bias=bias)
    return _unstack_pair(_softmax_pv(s, _row_max(s, sink), v, sink=sink))


def _prompt_attn_kernel(qa_ref, ka_ref, va_ref, qb_ref, kb_ref, vb_ref, bias_ref, sink_ref,
                        oa_ref, ob_ref, kap, vap, kbp, vbp, sa_scr, ma_scr, sb_scr, mb_scr, *, seq_rows, tq):
    i = pl.program_id(1)

    @pl.when(i == 0)
    def _():
        kap[0:A_WINDOW, :] = jnp.zeros((A_WINDOW, LANES), BF16)
        vap[0:A_WINDOW, :] = jnp.zeros((A_WINDOW, LANES), BF16)
        kbp[0:B_REACH, :] = jnp.zeros((B_REACH, B_W), BF16)
        vbp[0:B_REACH, :] = jnp.zeros((B_REACH, B_W), BF16)
        kbp[B_REACH + seq_rows:, :] = jnp.zeros((B_BAND_PAD - B_BAND, B_W), BF16)
        vbp[B_REACH + seq_rows:, :] = jnp.zeros((B_BAND_PAD - B_BAND, B_W), BF16)

        def copy(t, carry):
            r = pl.multiple_of(t * tq, tq)
            ra = pl.multiple_of(A_WINDOW + r, A_WINDOW)
            rb = pl.multiple_of(B_REACH + r, B_REACH)
            kap[pl.ds(ra, tq), :] = ka_ref[pl.ds(r, tq), :]
            vap[pl.ds(ra, tq), :] = va_ref[pl.ds(r, tq), :]
            kbp[pl.ds(rb, tq), :] = kb_ref[pl.ds(r, tq), :]
            vbp[pl.ds(rb, tq), :] = vb_ref[pl.ds(r, tq), :]
            return carry

        lax.fori_loop(0, seq_rows // tq, copy, 0)

    col_a = lax.broadcasted_iota(jnp.int32, (2 * CHUNK, A_BAND), 1)
    col_b = lax.broadcasted_iota(jnp.int32, (2 * CHUNK, B_BAND_PAD), 1)

    n_chunks = tq // CHUNK

    def band_start(cc):
        start = (i * n_chunks + cc) * CHUNK
        return start, pl.multiple_of(cc * CHUNK, CHUNK), pl.multiple_of(start, CHUNK)

    def score_stage(cc, buf, masked):
        start, r0, w0 = band_start(cc)
        valid_a = (col_a >= A_WINDOW - start) if masked else None
        valid_b = (col_b >= B_REACH - start) if masked else None
        for j in range(N_PAIRS):
            sl = slice(j * LANES, (j + 1) * LANES)
            s = _scores(_stack_pair(qa_ref[pl.ds(r0, CHUNK), sl]), kap[pl.ds(w0, A_BAND), :], valid=valid_a)
            sa_scr[buf, j] = s
            ma_scr[buf, j] = _row_max(s, sink_ref[j])
        for j in range(N_PAIRS):
            sl = slice(j * LANES, (j + 1) * LANES)
            s = _scores(_stack_pair(qb_ref[pl.ds(r0, CHUNK), sl]), kbp[pl.ds(w0, B_BAND_PAD), sl],
                        bias=bias_ref[j], valid=valid_b)
            sb_scr[buf, j] = s
            mb_scr[buf, j] = _row_max(s)

    def output_stage(cc, buf):
        _, r0, w0 = band_start(cc)
        for j in range(N_PAIRS):
            sl = slice(j * LANES, (j + 1) * LANES)
            o = _softmax_pv(sa_scr[buf, j], ma_scr[buf, j], vap[pl.ds(w0, A_BAND), :], sink=sink_ref[j])
            oa_ref[pl.ds(r0, CHUNK), sl] = _unstack_pair(o).astype(BF16)
        for j in range(N_PAIRS):
            sl = slice(j * LANES, (j + 1) * LANES)
            o = _softmax_pv(sb_scr[buf, j], mb_scr[buf, j], vbp[pl.ds(w0, B_BAND_PAD), sl])
            ob_ref[pl.ds(r0, CHUNK), sl] = _unstack_pair(o).astype(BF16)

    def run(masked):
        score_stage(0, 0, masked)

        def step(cc, carry):
            for parity in (0, 1):
                @pl.when(cc % 2 == parity)
                def _():
                    output_stage(cc, parity)
                    score_stage(cc + 1, 1 - parity, masked)
            return carry

        lax.fori_loop(0, n_chunks - 1, step, 0)
        output_stage(n_chunks - 1, (n_chunks - 1) % 2)

    @pl.when(i == 0)
    def _():
        run(True)

    @pl.when(i != 0)
    def _():
        run(False)


def _prompt_attention(qa, ka, va, qb, kb, vb, bias_pairs, sink_pairs, *, n_seq, seq_rows, tq):
    rows = n_seq * seq_rows
    nt = seq_rows // tq
    qspec = lambda width: pl.BlockSpec((tq, width), lambda b, i: (b * nt + i, 0))
    kvspec = lambda width: pl.BlockSpec((None, seq_rows, width), lambda b, i: (b, 0, 0))
    seq3 = lambda a: a.reshape(n_seq, seq_rows, a.shape[-1])
    return pl.pallas_call(
        functools.partial(_prompt_attn_kernel, seq_rows=seq_rows, tq=tq),
        grid=(n_seq, nt),
        in_specs=[qspec(A_Q_W), kvspec(A_KV_W), kvspec(A_KV_W), qspec(B_W), kvspec(B_W), kvspec(B_W),
                  _resident(bias_pairs.shape), _resident(sink_pairs.shape)],
        out_specs=[qspec(A_Q_W), qspec(B_W)],
        out_shape=[jax.ShapeDtypeStruct((rows, A_Q_W), BF16), jax.ShapeDtypeStruct((rows, B_W), BF16)],
        scratch_shapes=[pltpu.VMEM((A_WINDOW + seq_rows, A_KV_W), BF16),
                        pltpu.VMEM((A_WINDOW + seq_rows, A_KV_W), BF16),
                        pltpu.VMEM((B_REACH + seq_rows + B_BAND_PAD - B_BAND, B_W), BF16),
                        pltpu.VMEM((B_REACH + seq_rows + B_BAND_PAD - B_BAND, B_W), BF16),
                        pltpu.VMEM((2, N_PAIRS, 2 * CHUNK, A_BAND), F32),
                        pltpu.VMEM((2, N_PAIRS, 2 * CHUNK, LANES), F32),
                        pltpu.VMEM((2, N_PAIRS, 2 * CHUNK, B_BAND_PAD), F32),
                        pltpu.VMEM((2, N_PAIRS, 2 * CHUNK, LANES), F32)],
        compiler_params=pltpu.CompilerParams(
            dimension_semantics=("arbitrary", "arbitrary"), vmem_limit_bytes=VMEM_LIMIT),
        name="prompt_attention",
    )(qa, seq3(ka), seq3(va), qb, seq3(kb), seq3(vb), bias_pairs, sink_pairs)


def _sample_attn_kernel(qa_ref, ka_ref, va_ref, cak_ref, cav_ref, qb_ref, kb_ref, vb_ref, cbk_ref, cbv_ref,
                        bias_ref, sink_ref, oa_ref, ob_ref):
    ka = jnp.concatenate([cak_ref[...], ka_ref[...]], axis=0)
    va = jnp.concatenate([cav_ref[...], va_ref[...]], axis=0)
    for j in range(N_PAIRS):
        sl = slice(j * LANES, (j + 1) * LANES)
        oa_ref[:, sl] = _pair_attention(qa_ref[:, sl], ka, va, sink=sink_ref[j]).astype(BF16)
    for j in range(N_PAIRS):
        sl = slice(j * LANES, (j + 1) * LANES)
        kb = jnp.concatenate([cbk_ref[:, sl], kb_ref[:, sl]], axis=0)
        vb = jnp.concatenate([cbv_ref[:, sl], vb_ref[:, sl]], axis=0)
        ob_ref[:, sl] = _pair_attention(qb_ref[:, sl], kb, vb, bias=bias_ref[j]).astype(BF16)


def _sample_attention(qa, ka, va, cak, cav, qb, kb, vb, cbk, cbv, bias_pairs, sink_pairs, *, n_seq, seq_rows):
    rows = n_seq * seq_rows
    new = lambda width: pl.BlockSpec((seq_rows, width), lambda b: (b, 0))
    past = lambda a: pl.BlockSpec((None,) + a.shape[1:], lambda b: (b, 0, 0))
    return pl.pallas_call(
        _sample_attn_kernel,
        grid=(n_seq,),
        in_specs=[new(A_Q_W), new(A_KV_W), new(A_KV_W), past(cak), past(cav),
                  new(B_W), new(B_W), new(B_W), past(cbk), past(cbv),
                  _resident(bias_pairs.shape), _resident(sink_pairs.shape)],
        out_specs=[new(A_Q_W), new(B_W)],
        out_shape=[jax.ShapeDtypeStruct((rows, A_Q_W), BF16), jax.ShapeDtypeStruct((rows, B_W), BF16)],
        compiler_params=pltpu.CompilerParams(dimension_semantics=("arbitrary",)),
        name="sample_attention",
    )(qa, ka, va, cak, cav, qb, kb, vb, cbk, cbv, bias_pairs, sink_pairs)


def _ff_chunks(d_ff):
    return [(lo, min(FF_CHUNK, d_ff - lo)) for lo in range(0, d_ff, FF_CHUNK)]


def _interleave_up(w_up, d_ff):
    return jnp.concatenate([w_up[:, base + lo:base + lo + width]
                            for lo, width in _ff_chunks(d_ff) for base in (0, d_ff)], axis=1)


def _post_kernel(*refs, tm, d_ff, seq_rows, carry_conv):
    if carry_conv:
        (x_ref, oa_ref, ob_ref, sga_ref, sgb_ref, woa_ref, wob_ref, wout_ref, gpost_ref, gpre_ref,
         wup_ref, cw_ref, cb_ref, wdown_ref, gffn_ref, y_ref, conv_ref, h2_ref) = refs
    else:
        (x_ref, oa_ref, ob_ref, sga_ref, sgb_ref, woa_ref, wob_ref, wout_ref, gpost_ref, gpre_ref,
         wup_ref, cw_ref, cb_ref, wdown_ref, gffn_ref, p1_ref, p2_ref, y_ref, u_ref, h2_ref) = refs
    i = pl.program_id(1)

    if carry_conv:
        @pl.when(i == 0)
        def _():
            conv_ref[...] = jnp.zeros_like(conv_ref)

    mixed = (sga_ref[...].astype(F32) * jnp.dot(oa_ref[...], woa_ref[...], preferred_element_type=F32)
             + sgb_ref[...].astype(F32) * jnp.dot(ob_ref[...], wob_ref[...], preferred_element_type=F32))
    z = jnp.dot(mixed.astype(BF16), wout_ref[...], preferred_element_type=F32)
    x1 = x_ref[...] + _rms(z, gpost_ref[...])
    y_ref[...] = x1
    h2_ref[...] = _rms(x1, gpre_ref[...]).astype(BF16)

    sqrt_half = 0.7071067811865476
    ff = None
    for lo, width in _ff_chunks(d_ff):
        cols = slice(lo, lo + width)
        uv = jnp.dot(h2_ref[...], wup_ref[:, 2 * lo:2 * (lo + width)], preferred_element_type=F32)
        u = uv[:, :width]
        val = uv[:, width:]
        row = lax.broadcasted_iota(jnp.int32, (tm, width), 0)
        back1 = pltpu.roll(u, 1, axis=0)
        back2 = pltpu.roll(u, 2, axis=0)
        if carry_conv:
            c0 = conv_ref[0:1, cols]
            c1 = conv_ref[1:2, cols]
            back1 = jnp.where(row == 0, c1, back1)
            back2 = jnp.where(row == 0, c0, jnp.where(row == 1, c1, back2))
            conv_ref[:, cols] = u[tm - (CONV_WIDTH - 1):, :]
        else:
            row_in_seq = row % seq_rows
            back1 = jnp.where(row_in_seq == 0, p1_ref[:, cols], back1)
            back2 = jnp.where(row_in_seq < 2, p2_ref[:, cols], back2)
            u_ref[:, cols] = u
        conv = cb_ref[:, cols] + back2 * cw_ref[0:1, cols]
        conv = conv + back1 * cw_ref[1:2, cols]
        conv = conv + u * cw_ref[2:3, cols]
        act = 0.5 * conv * (1.0 + lax.erf(conv * sqrt_half))
        part = jnp.dot((act * val).astype(BF16), wdown_ref[cols, :], preferred_element_type=F32)
        ff = part if ff is None else ff + part
    y_ref[...] = y_ref[...] + _rms(ff, gffn_ref[...])


def _post(x2d, oa, ob, sga, sgb, w, *, n_seq, seq_rows, tm, conv_bounds=None, name):
    rows, d_model = x2d.shape
    d_ff = w["w_down"].shape[0]
    carry_conv = conv_bounds is None
    nt = seq_rows // tm if carry_conv else 1
    grid = (n_seq, nt) if carry_conv else (1, 1)
    tok = lambda width: pl.BlockSpec((tm, width), lambda b, i: (b * nt + i, 0))
    vec = lambda a: a.reshape(1, -1).astype(F32)
    weights = [w["w_oa"], w["w_ob"], w["w_out"], vec(w["g_mix_post"]), vec(w["g_ffn_pre"]), w["w_up"],
               w["conv_w"].astype(F32), vec(w["conv_b"]), w["w_down"], vec(w["g_ffn_post"])]
    in_specs = [tok(d_model), tok(A_Q_W), tok(B_W), tok(d_model), tok(d_model)] + [_resident(a.shape) for a in weights]
    args = [x2d, oa, ob, sga, sgb] + weights
    if carry_conv:
        side_spec = pl.BlockSpec((None, CONV_WIDTH - 1, d_ff), lambda b, i: (b, 0, 0))
        side_shape = jax.ShapeDtypeStruct((n_seq, CONV_WIDTH - 1, d_ff), F32)
    else:
        in_specs += [tok(d_ff), tok(d_ff)]
        args += list(conv_bounds)
        side_spec = tok(d_ff)
        side_shape = jax.ShapeDtypeStruct((rows, d_ff), F32)
    return pl.pallas_call(
        functools.partial(_post_kernel, tm=tm, d_ff=d_ff, seq_rows=seq_rows, carry_conv=carry_conv),
        grid=grid,
        in_specs=in_specs,
        out_specs=[tok(d_model), side_spec],
        out_shape=[jax.ShapeDtypeStruct((rows, d_model), F32), side_shape],
        scratch_shapes=[pltpu.VMEM((tm, d_model), BF16)],
        compiler_params=pltpu.CompilerParams(
            dimension_semantics=("arbitrary", "arbitrary"), vmem_limit_bytes=VMEM_LIMIT),
        name=name,
    )(*args)


def _rope_tables(pos):
    half = HEAD_DIM // 2
    inv = 1.0 / (ROPE_THETA ** (jnp.arange(half, dtype=F32) * (2.0 / HEAD_DIM)))
    ang = pos.astype(F32)[:, None] * inv[None, :]
    cos, sin = jnp.cos(ang), jnp.sin(ang)
    reps = LANES // HEAD_DIM
    return (jnp.tile(jnp.concatenate([cos, cos], axis=1), (1, reps)),
            jnp.tile(jnp.concatenate([-sin, sin], axis=1), (1, reps)))


def _pair_heads_cols(w_q):
    d = w_q.shape[0]
    per = A_Q_HEADS // A_KV_HEADS
    return w_q.reshape(d, A_KV_HEADS, per, HEAD_DIM).transpose(0, 2, 1, 3).reshape(d, A_Q_W)


def kernel(x_prompt, x_sample, cache_a_k, cache_a_v, cache_b_k, cache_b_v, state_conv, g_mix_pre, w_in, sinks,
           rel_bias, w_oa, w_ob, w_out, g_mix_post, g_ffn_pre, w_up, conv_w, conv_b, w_down, g_ffn_post):
    depth = w_in.shape[0]
    n_p, t_p, d_model = x_prompt.shape
    n_s, t_s, _ = x_sample.shape
    d_ff = w_down.shape[1]
    per = A_Q_HEADS // A_KV_HEADS
    tm = 512

    cos_p, sin_p = _rope_tables(jnp.arange(t_p))
    cos_s, sin_s = _rope_tables(jnp.tile(PAST_LEN + jnp.arange(t_s), n_s))

    yp = x_prompt.reshape(n_p * t_p, d_model)
    ys = x_sample.reshape(n_s * t_s, d_model)
    new_p, new_s = [], []
    for l in range(depth):
        w_in_bf = jnp.concatenate([_pair_heads_cols(w_in[l][:, :A_Q_W]), w_in[l][:, A_Q_W:]], axis=1).astype(BF16)
        w = dict(
            w_oa=w_oa[l].reshape(A_KV_HEADS, per, HEAD_DIM, d_model).transpose(1, 0, 2, 3)
                        .reshape(A_Q_W, d_model).astype(BF16),
            w_ob=w_ob[l].astype(BF16), w_out=w_out[l].astype(BF16), g_mix_post=g_mix_post[l],
            g_ffn_pre=g_ffn_pre[l], w_up=_interleave_up(w_up[l], d_ff).astype(BF16), conv_w=conv_w[l],
            conv_b=conv_b[l],
            w_down=w_down[l].astype(BF16), g_ffn_post=g_ffn_post[l])
        sink_heads = (sinks[l].astype(F32) * LOG2E).reshape(A_KV_HEADS, per).T.reshape(N_PAIRS, 2)

        def sink_rows(n):
            return jnp.broadcast_to(jnp.repeat(sink_heads, n, axis=1)[:, :, None], (N_PAIRS, 2 * n, LANES))

        bias = _build_bias(rel_bias[l])
        bias_p = bias.reshape(N_PAIRS, 2 * CHUNK, B_BAND_PAD)
        keys_s = cache_b_k.shape[2] + t_s
        bias_s = bias[:, :t_s, :keys_s].reshape(N_PAIRS, 2 * t_s, keys_s)

        keep_a, keep_b = min(A_WINDOW, t_p), min(B_REACH, t_p)
        (qa, ka, va, qb, kb, vb, sga, sgb, ak, av, bk, bv) = _in_projection(
            yp, g_mix_pre[l], w_in_bf, cos_p, sin_p, n_seq=n_p, seq_rows=t_p, tm=tm,
            keep_a=keep_a, keep_b=keep_b, name="prompt_in_projection")
        oa, ob = _prompt_attention(qa, ka, va, qb, kb, vb, bias_p, sink_rows(CHUNK),
                                   n_seq=n_p, seq_rows=t_p, tq=min(2 * tm, t_p))
        yp, conv_p = _post(yp, oa, ob, sga, sgb, w, n_seq=n_p, seq_rows=t_p, tm=tm, name="prompt_post")
        new_p.append((ak.reshape(n_p, keep_a, A_KV_HEADS, HEAD_DIM), av.reshape(n_p, keep_a, A_KV_HEADS, HEAD_DIM),
                      bk.reshape(n_p, keep_b, B_HEADS, HEAD_DIM), bv.reshape(n_p, keep_b, B_HEADS, HEAD_DIM),
                      conv_p))

        rows_s = n_s * t_s
        (qa, ka, va, qb, kb, vb, sga, sgb, ak, av, bk, bv) = _in_projection(
            ys, g_mix_pre[l], w_in_bf, cos_s, sin_s, n_seq=1, seq_rows=rows_s, tm=rows_s,
            keep_a=rows_s, keep_b=rows_s, name="sample_in_projection")
        flat_heads = lambda c: c.reshape(c.shape[0], c.shape[1], -1).astype(BF16)
        oa, ob = _sample_attention(qa, ka, va, flat_heads(cache_a_k[l]), flat_heads(cache_a_v[l]),
                                   qb, kb, vb, flat_heads(cache_b_k[l]), flat_heads(cache_b_v[l]),
                                   bias_s, sink_rows(t_s), n_seq=n_s, seq_rows=t_s)
        st = state_conv[l].astype(F32)
        zeros = jnp.zeros((n_s, t_s - 2, d_ff), F32)
        prev1 = jnp.concatenate([st[:, 1:2], zeros, zeros[:, :1]], axis=1).reshape(rows_s, d_ff)
        prev2 = jnp.concatenate([st, zeros], axis=1).reshape(rows_s, d_ff)
        ys, u_s = _post(ys, oa, ob, sga, sgb, w, n_seq=n_s, seq_rows=t_s, tm=rows_s,
                        conv_bounds=(prev1, prev2), name="sample_post")
        new_s.append((ak.reshape(n_s, t_s, A_KV_HEADS, HEAD_DIM), av.reshape(n_s, t_s, A_KV_HEADS, HEAD_DIM),
                      bk.reshape(n_s, t_s, B_HEADS, HEAD_DIM), bv.reshape(n_s, t_s, B_HEADS, HEAD_DIM),
                      u_s.reshape(n_s, t_s, d_ff)[:, t_s - (CONV_WIDTH - 1):]))

    stack = lambda lst, k: jnp.stack([s[k] for s in lst], axis=0)
    return (yp.reshape(n_p, t_p, d_model), ys.reshape(n_s, t_s, d_model),
            stack(new_p, 0), stack(new_p, 1), stack(new_p, 2), stack(new_p, 3), stack(new_p, 4),
            stack(new_s, 0), stack(new_s, 1), stack(new_s, 2), stack(new_s, 3), stack(new_s, 4))
```

```python
import functools

import jax
import jax.numpy as jnp
from jax import lax
from jax.experimental import pallas as pl
from jax.experimental.pallas import tpu as pltpu

CHUNK = 64
HEAD_DIM = 64
A_Q_HEADS = 8
A_KV_HEADS = 2
A_WINDOW = 128
B_HEADS = 8
B_REACH = 512
REL_CLIP = 128
ROPE_THETA = 10000.0
CONV_WIDTH = 3
RMS_EPS = 1e-6
NEG_INF = -1e30
PAST_LEN = 1024

A_Q_W = A_Q_HEADS * HEAD_DIM
A_KV_W = A_KV_HEADS * HEAD_DIM
B_W = B_HEADS * HEAD_DIM
A_BAND = A_WINDOW + CHUNK
B_BAND = B_REACH + CHUNK
B_BAND_PAD = 640
LOG2E = 1.4426950408889634
LANES = 128
N_PAIRS = A_Q_W // LANES
TOEPLITZ_W = 1024
PART_ROWS = 256
FF_CHUNK = 2816
VMEM_LIMIT = 56 * 1024 * 1024

BF16 = jnp.bfloat16
F32 = jnp.float32


def _resident(shape):
    nd = len(shape)
    return pl.BlockSpec(shape, lambda *_: (0,) * nd, pipeline_mode=pl.Buffered(1))


def _rms(x, g):
    return (x * lax.rsqrt(jnp.mean(x * x, axis=-1, keepdims=True) + RMS_EPS)) * g


def _bias_kernel(t_ref, o_ref):
    row = lax.broadcasted_iota(jnp.int32, (CHUNK, TOEPLITZ_W), 0)
    col = lax.broadcasted_iota(jnp.int32, (CHUNK, B_BAND_PAD), 1)
    for head in range(t_ref.shape[0]):
        x = jnp.broadcast_to(t_ref[head], (CHUNK, TOEPLITZ_W))
        for b in range(CHUNK.bit_length() - 1):
            x = jnp.where((row >> b) & 1 == 1, pltpu.roll(x, 1 << b, axis=1), x)
        o_ref[head] = jnp.where(col < B_BAND, x[:, :B_BAND_PAD], NEG_INF)


def _build_bias(rel_table):
    h = rel_table.shape[0]
    tab = rel_table.astype(F32) * LOG2E
    flat = B_REACH - REL_CLIP
    ramp = CHUNK + REL_CLIP
    row = jnp.concatenate([
        jnp.broadcast_to(tab[:, :1], (h, flat)),
        tab[:, :ramp],
        jnp.broadcast_to(tab[:, :1], (h, TOEPLITZ_W - flat - ramp)),
    ], axis=1).reshape(h, 1, TOEPLITZ_W)
    return pl.pallas_call(
        _bias_kernel,
        grid=(1,),
        in_specs=[pl.BlockSpec((h, 1, TOEPLITZ_W), lambda i: (0, 0, 0))],
        out_specs=pl.BlockSpec((h, CHUNK, B_BAND_PAD), lambda i: (0, 0, 0)),
        out_shape=jax.ShapeDtypeStruct((h, CHUNK, B_BAND_PAD), F32),
        name="rel_bias_tile",
    )(row)


def _rope(r, cos, sin_signed):
    lane = lax.broadcasted_iota(jnp.int32, (r.shape[0], LANES), 1)
    first_half = (lane & (HEAD_DIM // 2)) == 0
    out = []
    for j in range(r.shape[1] // LANES):
        blk = r[:, j * LANES:(j + 1) * LANES]
        up = pltpu.roll(blk, HEAD_DIM // 2, axis=1)
        down = pltpu.roll(blk, LANES - HEAD_DIM // 2, axis=1)
        out.append(blk * cos + jnp.where(first_half, down, up) * sin_signed)
    return out[0] if len(out) == 1 else jnp.concatenate(out, axis=1)


def _inproj_kernel(x_ref, g_ref, w_ref, cos_ref, sin_ref,
                   qa_ref, ka_ref, va_ref, qb_ref, kb_ref, vb_ref, sga_ref, sgb_ref,
                   ak_ref, av_ref, bk_ref, bv_ref, *, d_model, keep_a, keep_b, tm, n_split):
    i = pl.program_id(1)
    nt = pl.num_programs(1)
    scale = HEAD_DIM ** -0.5 * LOG2E
    part = tm // n_split

    def keep_tail(dst_ref, val, keep, r0):
        if keep <= tm:
            lo = max(r0, tm - keep)
            if lo < r0 + part:
                dst_ref[lo - (tm - keep):r0 + part - (tm - keep), :] = val[lo - r0:, :]
        else:
            first = nt - keep // tm

            @pl.when(i >= first)
            def _():
                dst_ref[pl.ds(pl.multiple_of((i - first) * tm + r0, part), part), :] = val

    for r0 in range(0, tm, part):
        rows = slice(r0, r0 + part)
        h = _rms(x_ref[rows, :], g_ref[...]).astype(BF16)
        r = jnp.dot(h, w_ref[...], preferred_element_type=F32)
        cos = cos_ref[rows, :]
        sin = sin_ref[rows, :]
        off = 0
        qa_ref[rows, :] = (_rope(r[:, off:off + A_Q_W], cos, sin) * scale).astype(BF16)
        off += A_Q_W
        ka = _rope(r[:, off:off + A_KV_W], cos, sin)
        ka_ref[rows, :] = ka.astype(BF16)
        keep_tail(ak_ref, ka, keep_a, r0)
        off += A_KV_W
        va = r[:, off:off + A_KV_W]
        va_ref[rows, :] = va.astype(BF16)
        keep_tail(av_ref, va, keep_a, r0)
        off += A_KV_W
        qb_ref[rows, :] = (r[:, off:off + B_W] * scale).astype(BF16)
        off += B_W
        kb = r[:, off:off + B_W]
        kb_ref[rows, :] = kb.astype(BF16)
        keep_tail(bk_ref, kb, keep_b, r0)
        off += B_W
        vb = r[:, off:off + B_W]
        vb_ref[rows, :] = vb.astype(BF16)
        keep_tail(bv_ref, vb, keep_b, r0)
        off += B_W
        sga_ref[rows, :] = jax.nn.sigmoid(r[:, off:off + d_model]).astype(BF16)
        off += d_model
        sgb_ref[rows, :] = jax.nn.sigmoid(r[:, off:off + d_model]).astype(BF16)


def _in_projection(x2d, g, w_in_bf, cos, sin, *, n_seq, seq_rows, tm, keep_a, keep_b, name):
    rows, d_model = x2d.shape
    nt = seq_rows // tm
    tok = lambda width: pl.BlockSpec((tm, width), lambda b, i: (b * nt + i, 0))
    tail = lambda keep, width: pl.BlockSpec((None, keep, width), lambda b, i: (b, 0, 0))
    tok_shape = lambda width: jax.ShapeDtypeStruct((rows, width), BF16)
    tail_shape = lambda keep, width: jax.ShapeDtypeStruct((n_seq, keep, width), F32)
    return pl.pallas_call(
        functools.partial(_inproj_kernel, d_model=d_model, keep_a=keep_a, keep_b=keep_b, tm=tm,
                          n_split=max(1, tm // PART_ROWS)),
        grid=(n_seq, nt),
        in_specs=[
            pl.BlockSpec((tm, d_model), lambda b, i: (b * nt + i, 0)),
            _resident((1, d_model)),
            _resident(w_in_bf.shape),
            pl.BlockSpec((tm, LANES), lambda b, i: (i, 0)),
            pl.BlockSpec((tm, LANES), lambda b, i: (i, 0)),
        ],
        out_specs=[tok(A_Q_W), tok(A_KV_W), tok(A_KV_W), tok(B_W), tok(B_W), tok(B_W),
                   tok(d_model), tok(d_model),
                   tail(keep_a, A_KV_W), tail(keep_a, A_KV_W), tail(keep_b, B_W), tail(keep_b, B_W)],
        out_shape=[tok_shape(A_Q_W), tok_shape(A_KV_W), tok_shape(A_KV_W), tok_shape(B_W), tok_shape(B_W),
                   tok_shape(B_W), tok_shape(d_model), tok_shape(d_model),
                   tail_shape(keep_a, A_KV_W), tail_shape(keep_a, A_KV_W),
                   tail_shape(keep_b, B_W), tail_shape(keep_b, B_W)],
        compiler_params=pltpu.CompilerParams(
            dimension_semantics=("arbitrary", "arbitrary"), vmem_limit_bytes=VMEM_LIMIT),
        name=name,
    )(x2d, g.reshape(1, d_model).astype(F32), w_in_bf, cos, sin)


def _stack_pair(q):
    lo = lax.broadcasted_iota(jnp.int32, q.shape, 1) < HEAD_DIM
    zero = jnp.zeros_like(q)
    return jnp.concatenate([jnp.where(lo, q, zero), jnp.where(lo, zero, q)], axis=0)


def _unstack_pair(o):
    n = o.shape[0] // 2
    lo = lax.broadcasted_iota(jnp.int32, (n, LANES), 1) < HEAD_DIM
    return jnp.where(lo, o[:n], o[n:])


def _scores(q2, k, *, bias=None, valid=None):
    s = lax.dot_general(q2, k, (((1,), (1,)), ((), ())), preferred_element_type=F32)
    if bias is not None:
        s = s + bias
    if valid is not None:
        s = jnp.where(valid, s, NEG_INF)
    return s


def _row_max(s, sink=None):
    m = jnp.broadcast_to(jnp.max(s, axis=-1, keepdims=True), (s.shape[0], LANES))
    return m if sink is None else jnp.maximum(m, sink)


def _softmax_terms(s, m, *, sink=None):
    nk = s.shape[1]
    p = jnp.concatenate([jnp.exp2(s[:, k:min(k + LANES, nk)] - m[:, :min(LANES, nk - k)])
                         for k in range(0, nk, LANES)], axis=1)
    denom = jnp.broadcast_to(jnp.sum(p, axis=-1, keepdims=True), m.shape)
    if sink is not None:
        denom = denom + jnp.exp2(sink - m)
    return p.astype(BF16), 1.0 / denom


def _softmax_pv(s, m, v, *, sink=None):
    p, recip = _softmax_terms(s, m, sink=sink)
    return jnp.dot(p, v, preferred_element_type=F32) * recip


def _pair_attention(q, k, v, *, bias=None, sink=None):
    s = _scores(_stack_pair(q), k, bias=bias)
    return _unstack_pair(_softmax_pv(s, _row_max(s, sink), v, sink=sink))


def _prompt_attn_kernel(qa_ref, ka_ref, va_ref, qb_ref, kb_ref, vb_ref, bias_ref, sink_ref,
                        oa_ref, ob_ref, kap, vap, kbp, vbp, sa_scr, ma_scr, sb_scr, mb_scr, *, seq_rows, tq):
    i = pl.program_id(1)

    @pl.when(i == 0)
    def _():
        kap[0:A_WINDOW, :] = jnp.zeros((A_WINDOW, LANES), BF16)
        vap[0:A_WINDOW, :] = jnp.zeros((A_WINDOW, LANES), BF16)
        kbp[0:B_REACH, :] = jnp.zeros((B_REACH, B_W), BF16)
        vbp[0:B_REACH, :] = jnp.zeros((B_REACH, B_W), BF16)
        kbp[B_REACH + seq_rows:, :] = jnp.zeros((B_BAND_PAD - B_BAND, B_W), BF16)
        vbp[B_REACH + seq_rows:, :] = jnp.zeros((B_BAND_PAD - B_BAND, B_W), BF16)

        def copy(t, carry):
            r = pl.multiple_of(t * tq, tq)
            ra = pl.multiple_of(A_WINDOW + r, A_WINDOW)
            rb = pl.multiple_of(B_REACH + r, B_REACH)
            kap[pl.ds(ra, tq), :] = ka_ref[pl.ds(r, tq), :]
            vap[pl.ds(ra, tq), :] = va_ref[pl.ds(r, tq), :]
            kbp[pl.ds(rb, tq), :] = kb_ref[pl.ds(r, tq), :]
            vbp[pl.ds(rb, tq), :] = vb_ref[pl.ds(r, tq), :]
            return carry

        lax.fori_loop(0, seq_rows // tq, copy, 0)

    col_a = lax.broadcasted_iota(jnp.int32, (2 * CHUNK, A_BAND), 1)
    col_b = lax.broadcasted_iota(jnp.int32, (2 * CHUNK, B_BAND_PAD), 1)

    n_chunks = tq // CHUNK

    def band_start(cc):
        start = (i * n_chunks + cc) * CHUNK
        return start, pl.multiple_of(cc * CHUNK, CHUNK), pl.multiple_of(start, CHUNK)

    def score_stage(cc, buf, masked):
        start, r0, w0 = band_start(cc)
        valid_a = (col_a >= A_WINDOW - start) if masked else None
        valid_b = (col_b >= B_REACH - start) if masked else None
        for j in range(N_PAIRS):
            sl = slice(j * LANES, (j + 1) * LANES)
            s = _scores(_stack_pair(qa_ref[pl.ds(r0, CHUNK), sl]), kap[pl.ds(w0, A_BAND), :], valid=valid_a)
            sa_scr[buf, j] = s
            ma_scr[buf, j] = _row_max(s, sink_ref[j])
        for j in range(N_PAIRS):
            sl = slice(j * LANES, (j + 1) * LANES)
            s = _scores(_stack_pair(qb_ref[pl.ds(r0, CHUNK), sl]), kbp[pl.ds(w0, B_BAND_PAD), sl],
                        bias=bias_ref[j], valid=valid_b)
            sb_scr[buf, j] = s
            mb_scr[buf, j] = _row_max(s)

    def output_stage(cc, buf):
        _, r0, w0 = band_start(cc)
        for j in range(N_PAIRS):
            sl = slice(j * LANES, (j + 1) * LANES)
            o = _softmax_pv(sa_scr[buf, j], ma_scr[buf, j], vap[pl.ds(w0, A_BAND), :], sink=sink_ref[j])
            oa_ref[pl.ds(r0, CHUNK), sl] = _unstack_pair(o).astype(BF16)
        for j in range(N_PAIRS):
            sl = slice(j * LANES, (j + 1) * LANES)
            o = _softmax_pv(sb_scr[buf, j], mb_scr[buf, j], vbp[pl.ds(w0, B_BAND_PAD), sl])
            ob_ref[pl.ds(r0, CHUNK), sl] = _unstack_pair(o).astype(BF16)

    def run(masked):
        n_steps = n_chunks // 2
        score_stage(0, 0, masked)
        score_stage(1, 1, masked)

        def step(t, carry):
            for parity in (0, 1):
                @pl.when(t % 2 == parity)
                def _():
                    output_stage(2 * t, 2 * parity)
                    output_stage(2 * t + 1, 2 * parity + 1)
                    score_stage(2 * t + 2, 2 - 2 * parity, masked)
                    score_stage(2 * t + 3, 3 - 2 * parity, masked)
            return carry

        lax.fori_loop(0, n_steps - 1, step, 0)
        output_stage(n_chunks - 2, 2 * ((n_steps - 1) % 2))
        output_stage(n_chunks - 1, 2 * ((n_steps - 1) % 2) + 1)

    @pl.when(i == 0)
    def _():
        run(True)

    @pl.when(i != 0)
    def _():
        run(False)


def _prompt_attention(qa, ka, va, qb, kb, vb, bias_pairs, sink_pairs, *, n_seq, seq_rows, tq):
    rows = n_seq * seq_rows
    nt = seq_rows // tq
    qspec = lambda width: pl.BlockSpec((tq, width), lambda b, i: (b * nt + i, 0))
    kvspec = lambda width: pl.BlockSpec((None, seq_rows, width), lambda b, i: (b, 0, 0))
    seq3 = lambda a: a.reshape(n_seq, seq_rows, a.shape[-1])
    return pl.pallas_call(
        functools.partial(_prompt_attn_kernel, seq_rows=seq_rows, tq=tq),
        grid=(n_seq, nt),
        in_specs=[qspec(A_Q_W), kvspec(A_KV_W), kvspec(A_KV_W), qspec(B_W), kvspec(B_W), kvspec(B_W),
                  _resident(bias_pairs.shape), _resident(sink_pairs.shape)],
        out_specs=[qspec(A_Q_W), qspec(B_W)],
        out_shape=[jax.ShapeDtypeStruct((rows, A_Q_W), BF16), jax.ShapeDtypeStruct((rows, B_W), BF16)],
        scratch_shapes=[pltpu.VMEM((A_WINDOW + seq_rows, A_KV_W), BF16),
                        pltpu.VMEM((A_WINDOW + seq_rows, A_KV_W), BF16),
                        pltpu.VMEM((B_REACH + seq_rows + B_BAND_PAD - B_BAND, B_W), BF16),
                        pltpu.VMEM((B_REACH + seq_rows + B_BAND_PAD - B_BAND, B_W), BF16),
                        pltpu.VMEM((4, N_PAIRS, 2 * CHUNK, A_BAND), F32),
                        pltpu.VMEM((4, N_PAIRS, 2 * CHUNK, LANES), F32),
                        pltpu.VMEM((4, N_PAIRS, 2 * CHUNK, B_BAND_PAD), F32),
                        pltpu.VMEM((4, N_PAIRS, 2 * CHUNK, LANES), F32)],
        compiler_params=pltpu.CompilerParams(
            dimension_semantics=("arbitrary", "arbitrary"), vmem_limit_bytes=VMEM_LIMIT),
        name="prompt_attention",
    )(qa, seq3(ka), seq3(va), qb, seq3(kb), seq3(vb), bias_pairs, sink_pairs)


def _sample_attn_kernel(qa_ref, ka_ref, va_ref, cak_ref, cav_ref, qb_ref, kb_ref, vb_ref, cbk_ref, cbv_ref,
                        bias_ref, sink_ref, oa_ref, ob_ref):
    ka = jnp.concatenate([cak_ref[...], ka_ref[...]], axis=0)
    va = jnp.concatenate([cav_ref[...], va_ref[...]], axis=0)
    for j in range(N_PAIRS):
        sl = slice(j * LANES, (j + 1) * LANES)
        oa_ref[:, sl] = _pair_attention(qa_ref[:, sl], ka, va, sink=sink_ref[j]).astype(BF16)
    for j in range(N_PAIRS):
        sl = slice(j * LANES, (j + 1) * LANES)
        kb = jnp.concatenate([cbk_ref[:, sl], kb_ref[:, sl]], axis=0)
        vb = jnp.concatenate([cbv_ref[:, sl], vb_ref[:, sl]], axis=0)
        ob_ref[:, sl] = _pair_attention(qb_ref[:, sl], kb, vb, bias=bias_ref[j]).astype(BF16)


def _sample_attention(qa, ka, va, cak, cav, qb, kb, vb, cbk, cbv, bias_pairs, sink_pairs, *, n_seq, seq_rows):
    rows = n_seq * seq_rows
    new = lambda width: pl.BlockSpec((seq_rows, width), lambda b: (b, 0))
    past = lambda a: pl.BlockSpec((None,) + a.shape[1:], lambda b: (b, 0, 0))
    return pl.pallas_call(
        _sample_attn_kernel,
        grid=(n_seq,),
        in_specs=[new(A_Q_W), new(A_KV_W), new(A_KV_W), past(cak), past(cav),
                  new(B_W), new(B_W), new(B_W), past(cbk), past(cbv),
                  _resident(bias_pairs.shape), _resident(sink_pairs.shape)],
        out_specs=[new(A_Q_W), new(B_W)],
        out_shape=[jax.ShapeDtypeStruct((rows, A_Q_W), BF16), jax.ShapeDtypeStruct((rows, B_W), BF16)],
        compiler_params=pltpu.CompilerParams(dimension_semantics=("arbitrary",)),
        name="sample_attention",
    )(qa, ka, va, cak, cav, qb, kb, vb, cbk, cbv, bias_pairs, sink_pairs)


def _ff_chunks(d_ff):
    return [(lo, min(FF_CHUNK, d_ff - lo)) for lo in range(0, d_ff, FF_CHUNK)]


def _interleave_up(w_up, d_ff):
    return jnp.concatenate([w_up[:, base + lo:base + lo + width]
                            for lo, width in _ff_chunks(d_ff) for base in (0, d_ff)], axis=1)


def _post_kernel(*refs, tm, d_ff, seq_rows, carry_conv):
    if carry_conv:
        (x_ref, oa_ref, ob_ref, sga_ref, sgb_ref, woa_ref, wob_ref, wout_ref, gpost_ref, gpre_ref,
         wup_ref, cw_ref, cb_ref, wdown_ref, gffn_ref, y_ref, conv_ref, h2_ref) = refs
    else:
        (x_ref, oa_ref, ob_ref, sga_ref, sgb_ref, woa_ref, wob_ref, wout_ref, gpost_ref, gpre_ref,
         wup_ref, cw_ref, cb_ref, wdown_ref, gffn_ref, p1_ref, p2_ref, y_ref, u_ref, h2_ref) = refs
    i = pl.program_id(1)

    if carry_conv:
        @pl.when(i == 0)
        def _():
            conv_ref[...] = jnp.zeros_like(conv_ref)

    mixed = (sga_ref[...].astype(F32) * jnp.dot(oa_ref[...], woa_ref[...], preferred_element_type=F32)
             + sgb_ref[...].astype(F32) * jnp.dot(ob_ref[...], wob_ref[...], preferred_element_type=F32))
    z = jnp.dot(mixed.astype(BF16), wout_ref[...], preferred_element_type=F32)
    x1 = x_ref[...] + _rms(z, gpost_ref[...])
    y_ref[...] = x1
    h2_ref[...] = _rms(x1, gpre_ref[...]).astype(BF16)

    sqrt_half = 0.7071067811865476
    ff = None
    for lo, width in _ff_chunks(d_ff):
        cols = slice(lo, lo + width)
        uv = jnp.dot(h2_ref[...], wup_ref[:, 2 * lo:2 * (lo + width)], preferred_element_type=F32)
        u = uv[:, :width]
        val = uv[:, width:]
        row = lax.broadcasted_iota(jnp.int32, (tm, width), 0)
        back1 = pltpu.roll(u, 1, axis=0)
        back2 = pltpu.roll(u, 2, axis=0)
        if carry_conv:
            c0 = conv_ref[0:1, cols]
            c1 = conv_ref[1:2, cols]
            back1 = jnp.where(row == 0, c1, back1)
            back2 = jnp.where(row == 0, c0, jnp.where(row == 1, c1, back2))
            conv_ref[:, cols] = u[tm - (CONV_WIDTH - 1):, :]
        else:
            row_in_seq = row % seq_rows
            back1 = jnp.where(row_in_seq == 0, p1_ref[:, cols], back1)
            back2 = jnp.where(row_in_seq < 2, p2_ref[:, cols], back2)
            u_ref[:, cols] = u
        conv = cb_ref[:, cols] + back2 * cw_ref[0:1, cols]
        conv = conv + back1 * cw_ref[1:2, cols]
        conv = conv + u * cw_ref[2:3, cols]
        act = 0.5 * conv * (1.0 + lax.erf(conv * sqrt_half))
        part = jnp.dot((act * val).astype(BF16), wdown_ref[cols, :], preferred_element_type=F32)
        ff = part if ff is None else ff + part
    y_ref[...] = y_ref[...] + _rms(ff, gffn_ref[...])


def _post(x2d, oa, ob, sga, sgb, w, *, n_seq, seq_rows, tm, conv_bounds=None, name):
    rows, d_model = x2d.shape
    d_ff = w["w_down"].shape[0]
    carry_conv = conv_bounds is None
    nt = seq_rows // tm if carry_conv else 1
    grid = (n_seq, nt) if carry_conv else (1, 1)
    tok = lambda width: pl.BlockSpec((tm, width), lambda b, i: (b * nt + i, 0))
    vec = lambda a: a.reshape(1, -1).astype(F32)
    weights = [w["w_oa"], w["w_ob"], w["w_out"], vec(w["g_mix_post"]), vec(w["g_ffn_pre"]), w["w_up"],
               w["conv_w"].astype(F32), vec(w["conv_b"]), w["w_down"], vec(w["g_ffn_post"])]
    in_specs = [tok(d_model), tok(A_Q_W), tok(B_W), tok(d_model), tok(d_model)] + [_resident(a.shape) for a in weights]
    args = [x2d, oa, ob, sga, sgb] + weights
    if carry_conv:
        side_spec = pl.BlockSpec((None, CONV_WIDTH - 1, d_ff), lambda b, i: (b, 0, 0))
        side_shape = jax.ShapeDtypeStruct((n_seq, CONV_WIDTH - 1, d_ff), F32)
    else:
        in_specs += [tok(d_ff), tok(d_ff)]
        args += list(conv_bounds)
        side_spec = tok(d_ff)
        side_shape = jax.ShapeDtypeStruct((rows, d_ff), F32)
    return pl.pallas_call(
        functools.partial(_post_kernel, tm=tm, d_ff=d_ff, seq_rows=seq_rows, carry_conv=carry_conv),
        grid=grid,
        in_specs=in_specs,
        out_specs=[tok(d_model), side_spec],
        out_shape=[jax.ShapeDtypeStruct((rows, d_model), F32), side_shape],
        scratch_shapes=[pltpu.VMEM((tm, d_model), BF16)],
        compiler_params=pltpu.CompilerParams(
            dimension_semantics=("arbitrary", "arbitrary"), vmem_limit_bytes=VMEM_LIMIT),
        name=name,
    )(*args)


def _rope_tables(pos):
    half = HEAD_DIM // 2
    inv = 1.0 / (ROPE_THETA ** (jnp.arange(half, dtype=F32) * (2.0 / HEAD_DIM)))
    ang = pos.astype(F32)[:, None] * inv[None, :]
    cos, sin = jnp.cos(ang), jnp.sin(ang)
    reps = LANES // HEAD_DIM
    return (jnp.tile(jnp.concatenate([cos, cos], axis=1), (1, reps)),
            jnp.tile(jnp.concatenate([-sin, sin], axis=1), (1, reps)))


def _pair_heads_cols(w_q):
    d = w_q.shape[0]
    per = A_Q_HEADS // A_KV_HEADS
    return w_q.reshape(d, A_KV_HEADS, per, HEAD_DIM).transpose(0, 2, 1, 3).reshape(d, A_Q_W)


def kernel(x_prompt, x_sample, cache_a_k, cache_a_v, cache_b_k, cache_b_v, state_conv, g_mix_pre, w_in, sinks,
           rel_bias, w_oa, w_ob, w_out, g_mix_post, g_ffn_pre, w_up, conv_w, conv_b, w_down, g_ffn_post):
    depth = w_in.shape[0]
    n_p, t_p, d_model = x_prompt.shape
    n_s, t_s, _ = x_sample.shape
    d_ff = w_down.shape[1]
    per = A_Q_HEADS // A_KV_HEADS
    tm = 512

    cos_p, sin_p = _rope_tables(jnp.arange(t_p))
    cos_s, sin_s = _rope_tables(jnp.tile(PAST_LEN + jnp.arange(t_s), n_s))

    yp = x_prompt.reshape(n_p * t_p, d_model)
    ys = x_sample.reshape(n_s * t_s, d_model)
    new_p, new_s = [], []
    for l in range(depth):
        w_in_bf = jnp.concatenate([_pair_heads_cols(w_in[l][:, :A_Q_W]), w_in[l][:, A_Q_W:]], axis=1).astype(BF16)
        w = dict(
            w_oa=w_oa[l].reshape(A_KV_HEADS, per, HEAD_DIM, d_model).transpose(1, 0, 2, 3)
                        .reshape(A_Q_W, d_model).astype(BF16),
            w_ob=w_ob[l].astype(BF16), w_out=w_out[l].astype(BF16), g_mix_post=g_mix_post[l],
            g_ffn_pre=g_ffn_pre[l], w_up=_interleave_up(w_up[l], d_ff).astype(BF16), conv_w=conv_w[l],
            conv_b=conv_b[l],
            w_down=w_down[l].astype(BF16), g_ffn_post=g_ffn_post[l])
        sink_heads = (sinks[l].astype(F32) * LOG2E).reshape(A_KV_HEADS, per).T.reshape(N_PAIRS, 2)

        def sink_rows(n):
            return jnp.broadcast_to(jnp.repeat(sink_heads, n, axis=1)[:, :, None], (N_PAIRS, 2 * n, LANES))

        bias = _build_bias(rel_bias[l])
        bias_p = bias.reshape(N_PAIRS, 2 * CHUNK, B_BAND_PAD)
        keys_s = cache_b_k.shape[2] + t_s
        bias_s = bias[:, :t_s, :keys_s].reshape(N_PAIRS, 2 * t_s, keys_s)

        keep_a, keep_b = min(A_WINDOW, t_p), min(B_REACH, t_p)
        (qa, ka, va, qb, kb, vb, sga, sgb, ak, av, bk, bv) = _in_projection(
            yp, g_mix_pre[l], w_in_bf, cos_p, sin_p, n_seq=n_p, seq_rows=t_p, tm=tm,
            keep_a=keep_a, keep_b=keep_b, name="prompt_in_projection")
        oa, ob = _prompt_attention(qa, ka, va, qb, kb, vb, bias_p, sink_rows(CHUNK),
                                   n_seq=n_p, seq_rows=t_p, tq=min(2 * tm, t_p))
        yp, conv_p = _post(yp, oa, ob, sga, sgb, w, n_seq=n_p, seq_rows=t_p, tm=tm, name="prompt_post")
        new_p.append((ak.reshape(n_p, keep_a, A_KV_HEADS, HEAD_DIM), av.reshape(n_p, keep_a, A_KV_HEADS, HEAD_DIM),
                      bk.reshape(n_p, keep_b, B_HEADS, HEAD_DIM), bv.reshape(n_p, keep_b, B_HEADS, HEAD_DIM),
                      conv_p))

        rows_s = n_s * t_s
        (qa, ka, va, qb, kb, vb, sga, sgb, ak, av, bk, bv) = _in_projection(
            ys, g_mix_pre[l], w_in_bf, cos_s, sin_s, n_seq=1, seq_rows=rows_s, tm=rows_s,
            keep_a=rows_s, keep_b=rows_s, name="sample_in_projection")
        flat_heads = lambda c: c.reshape(c.shape[0], c.shape[1], -1).astype(BF16)
        oa, ob = _sample_attention(qa, ka, va, flat_heads(cache_a_k[l]), flat_heads(cache_a_v[l]),
                                   qb, kb, vb, flat_heads(cache_b_k[l]), flat_heads(cache_b_v[l]),
                                   bias_s, sink_rows(t_s), n_seq=n_s, seq_rows=t_s)
        st = state_conv[l].astype(F32)
        zeros = jnp.zeros((n_s, t_s - 2, d_ff), F32)
        prev1 = jnp.concatenate([st[:, 1:2], zeros, zeros[:, :1]], axis=1).reshape(rows_s, d_ff)
        prev2 = jnp.concatenate([st, zeros], axis=1).reshape(rows_s, d_ff)
        ys, u_s = _post(ys, oa, ob, sga, sgb, w, n_seq=n_s, seq_rows=t_s, tm=rows_s,
                        conv_bounds=(prev1, prev2), name="sample_post")
        new_s.append((ak.reshape(n_s, t_s, A_KV_HEADS, HEAD_DIM), av.reshape(n_s, t_s, A_KV_HEADS, HEAD_DIM),
                      bk.reshape(n_s, t_s, B_HEADS, HEAD_DIM), bv.reshape(n_s, t_s, B_HEADS, HEAD_DIM),
                      u_s.reshape(n_s, t_s, d_ff)[:, t_s - (CONV_WIDTH - 1):]))

    stack = lambda lst, k: jnp.stack([s[k] for s in lst], axis=0)
    return (yp.reshape(n_p, t_p, d_model), ys.reshape(n_s, t_s, d_model),
            stack(new_p, 0), stack(new_p, 1), stack(new_p, 2), stack(new_p, 3), stack(new_p, 4),
            stack(new_s, 0), stack(new_s, 1), stack(new_s, 2), stack(new_s, 3), stack(new_s, 4))
```

```python
import functools

import jax
import jax.numpy as jnp
from jax import lax
from jax.experimental import pallas as pl
from jax.experimental.pallas import tpu as pltpu

CHUNK = 64
HEAD_DIM = 64
A_Q_HEADS = 8
A_KV_HEADS = 2
A_WINDOW = 128
B_HEADS = 8
B_REACH = 512
REL_CLIP = 128
ROPE_THETA = 10000.0
CONV_WIDTH = 3
RMS_EPS = 1e-6
NEG_INF = -1e30
PAST_LEN = 1024

A_Q_W = A_Q_HEADS * HEAD_DIM
A_KV_W = A_KV_HEADS * HEAD_DIM
B_W = B_HEADS * HEAD_DIM
A_BAND = A_WINDOW + CHUNK
B_BAND = B_REACH + CHUNK
B_BAND_PAD = 640
LOG2E = 1.4426950408889634
LANES = 128
N_PAIRS = A_Q_W // LANES
TOEPLITZ_W = 1024
STEP_CHUNKS = 4
PART_ROWS = 256
FF_CHUNK = 2816
VMEM_LIMIT = 56 * 1024 * 1024

BF16 = jnp.bfloat16
F32 = jnp.float32


def _resident(shape):
    nd = len(shape)
    return pl.BlockSpec(shape, lambda *_: (0,) * nd, pipeline_mode=pl.Buffered(1))


def _rms(x, g):
    return (x * lax.rsqrt(jnp.mean(x * x, axis=-1, keepdims=True) + RMS_EPS)) * g


def _bias_kernel(t_ref, o_ref):
    row = lax.broadcasted_iota(jnp.int32, (CHUNK, TOEPLITZ_W), 0)
    col = lax.broadcasted_iota(jnp.int32, (CHUNK, B_BAND_PAD), 1)
    for head in range(t_ref.shape[0]):
        x = jnp.broadcast_to(t_ref[head], (CHUNK, TOEPLITZ_W))
        for b in range(CHUNK.bit_length() - 1):
            x = jnp.where((row >> b) & 1 == 1, pltpu.roll(x, 1 << b, axis=1), x)
        o_ref[head] = jnp.where(col < B_BAND, x[:, :B_BAND_PAD], NEG_INF)


def _build_bias(rel_table):
    h = rel_table.shape[0]
    tab = rel_table.astype(F32) * LOG2E
    flat = B_REACH - REL_CLIP
    ramp = CHUNK + REL_CLIP
    row = jnp.concatenate([
        jnp.broadcast_to(tab[:, :1], (h, flat)),
        tab[:, :ramp],
        jnp.broadcast_to(tab[:, :1], (h, TOEPLITZ_W - flat - ramp)),
    ], axis=1).reshape(h, 1, TOEPLITZ_W)
    return pl.pallas_call(
        _bias_kernel,
        grid=(1,),
        in_specs=[pl.BlockSpec((h, 1, TOEPLITZ_W), lambda i: (0, 0, 0))],
        out_specs=pl.BlockSpec((h, CHUNK, B_BAND_PAD), lambda i: (0, 0, 0)),
        out_shape=jax.ShapeDtypeStruct((h, CHUNK, B_BAND_PAD), F32),
        name="rel_bias_tile",
    )(row)


def _rope(r, cos, sin_signed):
    lane = lax.broadcasted_iota(jnp.int32, (r.shape[0], LANES), 1)
    first_half = (lane & (HEAD_DIM // 2)) == 0
    out = []
    for j in range(r.shape[1] // LANES):
        blk = r[:, j * LANES:(j + 1) * LANES]
        up = pltpu.roll(blk, HEAD_DIM // 2, axis=1)
        down = pltpu.roll(blk, LANES - HEAD_DIM // 2, axis=1)
        out.append(blk * cos + jnp.where(first_half, down, up) * sin_signed)
    return out[0] if len(out) == 1 else jnp.concatenate(out, axis=1)


def _inproj_kernel(x_ref, g_ref, w_ref, cos_ref, sin_ref,
                   qa_ref, ka_ref, va_ref, qb_ref, kb_ref, vb_ref, sga_ref, sgb_ref,
                   ak_ref, av_ref, bk_ref, bv_ref, *, d_model, keep_a, keep_b, tm, n_split):
    i = pl.program_id(1)
    nt = pl.num_programs(1)
    scale = HEAD_DIM ** -0.5 * LOG2E
    part = tm // n_split

    def keep_tail(dst_ref, val, keep, r0):
        if keep <= tm:
            lo = max(r0, tm - keep)
            if lo < r0 + part:
                dst_ref[lo - (tm - keep):r0 + part - (tm - keep), :] = val[lo - r0:, :]
        else:
            first = nt - keep // tm

            @pl.when(i >= first)
            def _():
                dst_ref[pl.ds(pl.multiple_of((i - first) * tm + r0, part), part), :] = val

    for r0 in range(0, tm, part):
        rows = slice(r0, r0 + part)
        h = _rms(x_ref[rows, :], g_ref[...]).astype(BF16)
        r = jnp.dot(h, w_ref[...], preferred_element_type=F32)
        cos = cos_ref[rows, :]
        sin = sin_ref[rows, :]
        off = 0
        qa_ref[rows, :] = (_rope(r[:, off:off + A_Q_W], cos, sin) * scale).astype(BF16)
        off += A_Q_W
        ka = _rope(r[:, off:off + A_KV_W], cos, sin)
        ka_ref[rows, :] = ka.astype(BF16)
        keep_tail(ak_ref, ka, keep_a, r0)
        off += A_KV_W
        va = r[:, off:off + A_KV_W]
        va_ref[rows, :] = va.astype(BF16)
        keep_tail(av_ref, va, keep_a, r0)
        off += A_KV_W
        qb_ref[rows, :] = (r[:, off:off + B_W] * scale).astype(BF16)
        off += B_W
        kb = r[:, off:off + B_W]
        kb_ref[rows, :] = kb.astype(BF16)
        keep_tail(bk_ref, kb, keep_b, r0)
        off += B_W
        vb = r[:, off:off + B_W]
        vb_ref[rows, :] = vb.astype(BF16)
        keep_tail(bv_ref, vb, keep_b, r0)
        off += B_W
        sga_ref[rows, :] = jax.nn.sigmoid(r[:, off:off + d_model]).astype(BF16)
        off += d_model
        sgb_ref[rows, :] = jax.nn.sigmoid(r[:, off:off + d_model]).astype(BF16)


def _in_projection(x2d, g, w_in_bf, cos, sin, *, n_seq, seq_rows, tm, keep_a, keep_b, name):
    rows, d_model = x2d.shape
    nt = seq_rows // tm
    tok = lambda width: pl.BlockSpec((tm, width), lambda b, i: (b * nt + i, 0))
    tail = lambda keep, width: pl.BlockSpec((None, keep, width), lambda b, i: (b, 0, 0))
    tok_shape = lambda width: jax.ShapeDtypeStruct((rows, width), BF16)
    tail_shape = lambda keep, width: jax.ShapeDtypeStruct((n_seq, keep, width), F32)
    return pl.pallas_call(
        functools.partial(_inproj_kernel, d_model=d_model, keep_a=keep_a, keep_b=keep_b, tm=tm,
                          n_split=max(1, tm // PART_ROWS)),
        grid=(n_seq, nt),
        in_specs=[
            pl.BlockSpec((tm, d_model), lambda b, i: (b * nt + i, 0)),
            _resident((1, d_model)),
            _resident(w_in_bf.shape),
            pl.BlockSpec((tm, LANES), lambda b, i: (i, 0)),
            pl.BlockSpec((tm, LANES), lambda b, i: (i, 0)),
        ],
        out_specs=[tok(A_Q_W), tok(A_KV_W), tok(A_KV_W), tok(B_W), tok(B_W), tok(B_W),
                   tok(d_model), tok(d_model),
                   tail(keep_a, A_KV_W), tail(keep_a, A_KV_W), tail(keep_b, B_W), tail(keep_b, B_W)],
        out_shape=[tok_shape(A_Q_W), tok_shape(A_KV_W), tok_shape(A_KV_W), tok_shape(B_W), tok_shape(B_W),
                   tok_shape(B_W), tok_shape(d_model), tok_shape(d_model),
                   tail_shape(keep_a, A_KV_W), tail_shape(keep_a, A_KV_W),
                   tail_shape(keep_b, B_W), tail_shape(keep_b, B_W)],
        compiler_params=pltpu.CompilerParams(
            dimension_semantics=("arbitrary", "arbitrary"), vmem_limit_bytes=VMEM_LIMIT),
        name=name,
    )(x2d, g.reshape(1, d_model).astype(F32), w_in_bf, cos, sin)


def _stack_pair(q):
    lo = lax.broadcasted_iota(jnp.int32, q.shape, 1) < HEAD_DIM
    zero = jnp.zeros_like(q)
    return jnp.concatenate([jnp.where(lo, q, zero), jnp.where(lo, zero, q)], axis=0)


def _unstack_pair(o):
    n = o.shape[0] // 2
    lo = lax.broadcasted_iota(jnp.int32, (n, LANES), 1) < HEAD_DIM
    return jnp.where(lo, o[:n], o[n:])


def _scores(q2, k, *, bias=None, valid=None):
    s = lax.dot_general(q2, k, (((1,), (1,)), ((), ())), preferred_element_type=F32)
    if bias is not None:
        s = s + bias
    if valid is not None:
        s = jnp.where(valid, s, NEG_INF)
    return s


def _row_max(s, sink=None):
    m = jnp.broadcast_to(jnp.max(s, axis=-1, keepdims=True), (s.shape[0], LANES))
    return m if sink is None else jnp.maximum(m, sink)


def _softmax_terms(s, m, *, sink=None):
    nk = s.shape[1]
    p = jnp.concatenate([jnp.exp2(s[:, k:min(k + LANES, nk)] - m[:, :min(LANES, nk - k)])
                         for k in range(0, nk, LANES)], axis=1)
    denom = jnp.broadcast_to(jnp.sum(p, axis=-1, keepdims=True), m.shape)
    if sink is not None:
        denom = denom + jnp.exp2(sink - m)
    return p.astype(BF16), 1.0 / denom


def _softmax_pv(s, m, v, *, sink=None):
    p, recip = _softmax_terms(s, m, sink=sink)
    return jnp.dot(p, v, preferred_element_type=F32) * recip


def _pair_attention(q, k, v, *, bias=None, sink=None):
    s = _scores(_stack_pair(q), k, bias=bias)
    return _unstack_pair(_softmax_pv(s, _row_max(s, sink), v, sink=sink))


def _prompt_attn_kernel(qa_ref, ka_ref, va_ref, qb_ref, kb_ref, vb_ref, bias_ref, sink_ref,
                        oa_ref, ob_ref, kap_all, vap_all, kbp_all, vbp_all, kv_sem,
                        sa_scr, ma_scr, sb_scr, mb_scr, *, seq_rows, tq):
    b = pl.program_id(0)
    i = pl.program_id(1)
    n_seq = pl.num_programs(0)
    slot = b % 2

    def kv_copies(seq, dst_slot):
        pairs = [(ka_ref, kap_all, A_WINDOW), (va_ref, vap_all, A_WINDOW),
                 (kb_ref, kbp_all, B_REACH), (vb_ref, vbp_all, B_REACH)]
        return [pltpu.make_async_copy(src.at[seq], dst.at[dst_slot, pl.ds(pad, seq_rows), :],
                                      kv_sem.at[dst_slot, n])
                for n, (src, dst, pad) in enumerate(pairs)]

    @pl.when(i == 0)
    def _():
        @pl.when(b == 0)
        def _():
            for s in range(2):
                kap_all[s, 0:A_WINDOW, :] = jnp.zeros((A_WINDOW, LANES), BF16)
                vap_all[s, 0:A_WINDOW, :] = jnp.zeros((A_WINDOW, LANES), BF16)
                kbp_all[s, 0:B_REACH, :] = jnp.zeros((B_REACH, B_W), BF16)
                vbp_all[s, 0:B_REACH, :] = jnp.zeros((B_REACH, B_W), BF16)
                kbp_all[s, B_REACH + seq_rows:, :] = jnp.zeros((B_BAND_PAD - B_BAND, B_W), BF16)
                vbp_all[s, B_REACH + seq_rows:, :] = jnp.zeros((B_BAND_PAD - B_BAND, B_W), BF16)
            for copy in kv_copies(0, 0):
                copy.start()

        for copy in kv_copies(b, slot):
            copy.wait()

        @pl.when(b + 1 < n_seq)
        def _():
            for copy in kv_copies(b + 1, 1 - slot):
                copy.start()

    kap, vap, kbp, vbp = kap_all.at[slot], vap_all.at[slot], kbp_all.at[slot], vbp_all.at[slot]

    col_a = lax.broadcasted_iota(jnp.int32, (2 * CHUNK, A_BAND), 1)
    col_b = lax.broadcasted_iota(jnp.int32, (2 * CHUNK, B_BAND_PAD), 1)

    n_chunks = tq // CHUNK

    def band_start(cc):
        start = (i * n_chunks + cc) * CHUNK
        return start, pl.multiple_of(cc * CHUNK, CHUNK), pl.multiple_of(start, CHUNK)

    def score_stage(cc, buf, masked):
        start, r0, w0 = band_start(cc)
        valid_a = (col_a >= A_WINDOW - start) if masked else None
        valid_b = (col_b >= B_REACH - start) if masked else None
        for j in range(N_PAIRS):
            sl = slice(j * LANES, (j + 1) * LANES)
            s = _scores(_stack_pair(qa_ref[pl.ds(r0, CHUNK), sl]), kap[pl.ds(w0, A_BAND), :], valid=valid_a)
            sa_scr[buf, j] = s
            ma_scr[buf, j] = _row_max(s, sink_ref[j])
        for j in range(N_PAIRS):
            sl = slice(j * LANES, (j + 1) * LANES)
            s = _scores(_stack_pair(qb_ref[pl.ds(r0, CHUNK), sl]), kbp[pl.ds(w0, B_BAND_PAD), sl],
                        bias=bias_ref[j], valid=valid_b)
            sb_scr[buf, j] = s
            mb_scr[buf, j] = _row_max(s)

    def output_stage(cc, buf):
        _, r0, w0 = band_start(cc)
        for j in range(N_PAIRS):
            sl = slice(j * LANES, (j + 1) * LANES)
            o = _softmax_pv(sa_scr[buf, j], ma_scr[buf, j], vap[pl.ds(w0, A_BAND), :], sink=sink_ref[j])
            oa_ref[pl.ds(r0, CHUNK), sl] = _unstack_pair(o).astype(BF16)
        for j in range(N_PAIRS):
            sl = slice(j * LANES, (j + 1) * LANES)
            o = _softmax_pv(sb_scr[buf, j], mb_scr[buf, j], vbp[pl.ds(w0, B_BAND_PAD), sl])
            ob_ref[pl.ds(r0, CHUNK), sl] = _unstack_pair(o).astype(BF16)

    def run(masked):
        k = STEP_CHUNKS
        n_steps = n_chunks // k
        for c in range(k):
            score_stage(c, c, masked)

        def step(t, carry):
            for parity in (0, 1):
                @pl.when(t % 2 == parity)
                def _():
                    for c in range(k):
                        output_stage(k * t + c, k * parity + c)
                    for c in range(k):
                        score_stage(k * (t + 1) + c, k * (1 - parity) + c, masked)
            return carry

        lax.fori_loop(0, n_steps - 1, step, 0)
        for c in range(k):
            output_stage(n_chunks - k + c, k * ((n_steps - 1) % 2) + c)

    @pl.when(i == 0)
    def _():
        run(True)

    @pl.when(i != 0)
    def _():
        run(False)


def _prompt_attention(qa, ka, va, qb, kb, vb, bias_pairs, sink_pairs, *, n_seq, seq_rows, tq):
    rows = n_seq * seq_rows
    nt = seq_rows // tq
    qspec = lambda width: pl.BlockSpec((tq, width), lambda b, i: (b * nt + i, 0))
    kvspec = pl.BlockSpec(memory_space=pl.ANY)
    seq3 = lambda a: a.reshape(n_seq, seq_rows, a.shape[-1])
    return pl.pallas_call(
        functools.partial(_prompt_attn_kernel, seq_rows=seq_rows, tq=tq),
        grid=(n_seq, nt),
        in_specs=[qspec(A_Q_W), kvspec, kvspec, qspec(B_W), kvspec, kvspec,
                  _resident(bias_pairs.shape), _resident(sink_pairs.shape)],
        out_specs=[qspec(A_Q_W), qspec(B_W)],
        out_shape=[jax.ShapeDtypeStruct((rows, A_Q_W), BF16), jax.ShapeDtypeStruct((rows, B_W), BF16)],
        scratch_shapes=[pltpu.VMEM((2, A_WINDOW + seq_rows, A_KV_W), BF16),
                        pltpu.VMEM((2, A_WINDOW + seq_rows, A_KV_W), BF16),
                        pltpu.VMEM((2, B_REACH + seq_rows + B_BAND_PAD - B_BAND, B_W), BF16),
                        pltpu.VMEM((2, B_REACH + seq_rows + B_BAND_PAD - B_BAND, B_W), BF16),
                        pltpu.SemaphoreType.DMA((2, 4)),
                        pltpu.VMEM((2 * STEP_CHUNKS, N_PAIRS, 2 * CHUNK, A_BAND), F32),
                        pltpu.VMEM((2 * STEP_CHUNKS, N_PAIRS, 2 * CHUNK, LANES), F32),
                        pltpu.VMEM((2 * STEP_CHUNKS, N_PAIRS, 2 * CHUNK, B_BAND_PAD), F32),
                        pltpu.VMEM((2 * STEP_CHUNKS, N_PAIRS, 2 * CHUNK, LANES), F32)],
        compiler_params=pltpu.CompilerParams(
            dimension_semantics=("arbitrary", "arbitrary"), vmem_limit_bytes=VMEM_LIMIT),
        name="prompt_attention",
    )(qa, seq3(ka), seq3(va), qb, seq3(kb), seq3(vb), bias_pairs, sink_pairs)


def _sample_attn_kernel(qa_ref, ka_ref, va_ref, cak_ref, cav_ref, qb_ref, kb_ref, vb_ref, cbk_ref, cbv_ref,
                        bias_ref, sink_ref, oa_ref, ob_ref):
    ka = jnp.concatenate([cak_ref[...], ka_ref[...]], axis=0)
    va = jnp.concatenate([cav_ref[...], va_ref[...]], axis=0)
    for j in range(N_PAIRS):
        sl = slice(j * LANES, (j + 1) * LANES)
        oa_ref[:, sl] = _pair_attention(qa_ref[:, sl], ka, va, sink=sink_ref[j]).astype(BF16)
    for j in range(N_PAIRS):
        sl = slice(j * LANES, (j + 1) * LANES)
        kb = jnp.concatenate([cbk_ref[:, sl], kb_ref[:, sl]], axis=0)
        vb = jnp.concatenate([cbv_ref[:, sl], vb_ref[:, sl]], axis=0)
        ob_ref[:, sl] = _pair_attention(qb_ref[:, sl], kb, vb, bias=bias_ref[j]).astype(BF16)


def _sample_attention(qa, ka, va, cak, cav, qb, kb, vb, cbk, cbv, bias_pairs, sink_pairs, *, n_seq, seq_rows):
    rows = n_seq * seq_rows
    new = lambda width: pl.BlockSpec((seq_rows, width), lambda b: (b, 0))
    past = lambda a: pl.BlockSpec((None,) + a.shape[1:], lambda b: (b, 0, 0))
    return pl.pallas_call(
        _sample_attn_kernel,
        grid=(n_seq,),
        in_specs=[new(A_Q_W), new(A_KV_W), new(A_KV_W), past(cak), past(cav),
                  new(B_W), new(B_W), new(B_W), past(cbk), past(cbv),
                  _resident(bias_pairs.shape), _resident(sink_pairs.shape)],
        out_specs=[new(A_Q_W), new(B_W)],
        out_shape=[jax.ShapeDtypeStruct((rows, A_Q_W), BF16), jax.ShapeDtypeStruct((rows, B_W), BF16)],
        compiler_params=pltpu.CompilerParams(dimension_semantics=("arbitrary",)),
        name="sample_attention",
    )(qa, ka, va, cak, cav, qb, kb, vb, cbk, cbv, bias_pairs, sink_pairs)


def _ff_chunks(d_ff):
    return [(lo, min(FF_CHUNK, d_ff - lo)) for lo in range(0, d_ff, FF_CHUNK)]


def _interleave_up(w_up, d_ff):
    return jnp.concatenate([w_up[:, base + lo:base + lo + width]
                            for lo, width in _ff_chunks(d_ff) for base in (0, d_ff)], axis=1)


def _post_kernel(*refs, tm, d_ff, seq_rows, carry_conv):
    if carry_conv:
        (x_ref, oa_ref, ob_ref, sga_ref, sgb_ref, woa_ref, wob_ref, wout_ref, gpost_ref, gpre_ref,
         wup_ref, cw_ref, cb_ref, wdown_ref, gffn_ref, y_ref, conv_ref, h2_ref) = refs
    else:
        (x_ref, oa_ref, ob_ref, sga_ref, sgb_ref, woa_ref, wob_ref, wout_ref, gpost_ref, gpre_ref,
         wup_ref, cw_ref, cb_ref, wdown_ref, gffn_ref, p1_ref, p2_ref, y_ref, u_ref, h2_ref) = refs
    i = pl.program_id(1)

    if carry_conv:
        @pl.when(i == 0)
        def _():
            conv_ref[...] = jnp.zeros_like(conv_ref)

    mixed = (sga_ref[...].astype(F32) * jnp.dot(oa_ref[...], woa_ref[...], preferred_element_type=F32)
             + sgb_ref[...].astype(F32) * jnp.dot(ob_ref[...], wob_ref[...], preferred_element_type=F32))
    z = jnp.dot(mixed.astype(BF16), wout_ref[...], preferred_element_type=F32)
    x1 = x_ref[...] + _rms(z, gpost_ref[...])
    y_ref[...] = x1
    h2_ref[...] = _rms(x1, gpre_ref[...]).astype(BF16)

    sqrt_half = 0.7071067811865476
    ff = None
    for lo, width in _ff_chunks(d_ff):
        cols = slice(lo, lo + width)
        uv = jnp.dot(h2_ref[...], wup_ref[:, 2 * lo:2 * (lo + width)], preferred_element_type=F32)
        u = uv[:, :width]
        val = uv[:, width:]
        row = lax.broadcasted_iota(jnp.int32, (tm, width), 0)
        back1 = pltpu.roll(u, 1, axis=0)
        back2 = pltpu.roll(u, 2, axis=0)
        if carry_conv:
            c0 = conv_ref[0:1, cols]
            c1 = conv_ref[1:2, cols]
            back1 = jnp.where(row == 0, c1, back1)
            back2 = jnp.where(row == 0, c0, jnp.where(row == 1, c1, back2))
            conv_ref[:, cols] = u[tm - (CONV_WIDTH - 1):, :]
        else:
            row_in_seq = row % seq_rows
            back1 = jnp.where(row_in_seq == 0, p1_ref[:, cols], back1)
            back2 = jnp.where(row_in_seq < 2, p2_ref[:, cols], back2)
            u_ref[:, cols] = u
        conv = cb_ref[:, cols] + back2 * cw_ref[0:1, cols]
        conv = conv + back1 * cw_ref[1:2, cols]
        conv = conv + u * cw_ref[2:3, cols]
        act = 0.5 * conv * (1.0 + lax.erf(conv * sqrt_half))
        part = jnp.dot((act * val).astype(BF16), wdown_ref[cols, :], preferred_element_type=F32)
        ff = part if ff is None else ff + part
    y_ref[...] = y_ref[...] + _rms(ff, gffn_ref[...])


def _post(x2d, oa, ob, sga, sgb, w, *, n_seq, seq_rows, tm, conv_bounds=None, name):
    rows, d_model = x2d.shape
    d_ff = w["w_down"].shape[0]
    carry_conv = conv_bounds is None
    nt = seq_rows // tm if carry_conv else 1
    grid = (n_seq, nt) if carry_conv else (1, 1)
    tok = lambda width: pl.BlockSpec((tm, width), lambda b, i: (b * nt + i, 0))
    vec = lambda a: a.reshape(1, -1).astype(F32)
    weights = [w["w_oa"], w["w_ob"], w["w_out"], vec(w["g_mix_post"]), vec(w["g_ffn_pre"]), w["w_up"],
               w["conv_w"].astype(F32), vec(w["conv_b"]), w["w_down"], vec(w["g_ffn_post"])]
    in_specs = [tok(d_model), tok(A_Q_W), tok(B_W), tok(d_model), tok(d_model)] + [_resident(a.shape) for a in weights]
    args = [x2d, oa, ob, sga, sgb] + weights
    if carry_conv:
        side_spec = pl.BlockSpec((None, CONV_WIDTH - 1, d_ff), lambda b, i: (b, 0, 0))
        side_shape = jax.ShapeDtypeStruct((n_seq, CONV_WIDTH - 1, d_ff), F32)
    else:
        in_specs += [tok(d_ff), tok(d_ff)]
        args += list(conv_bounds)
        side_spec = tok(d_ff)
        side_shape = jax.ShapeDtypeStruct((rows, d_ff), F32)
    return pl.pallas_call(
        functools.partial(_post_kernel, tm=tm, d_ff=d_ff, seq_rows=seq_rows, carry_conv=carry_conv),
        grid=grid,
        in_specs=in_specs,
        out_specs=[tok(d_model), side_spec],
        out_shape=[jax.ShapeDtypeStruct((rows, d_model), F32), side_shape],
        scratch_shapes=[pltpu.VMEM((tm, d_model), BF16)],
        compiler_params=pltpu.CompilerParams(
            dimension_semantics=("arbitrary", "arbitrary"), vmem_limit_bytes=VMEM_LIMIT),
        name=name,
    )(*args)


def _rope_tables(pos):
    half = HEAD_DIM // 2
    inv = 1.0 / (ROPE_THETA ** (jnp.arange(half, dtype=F32) * (2.0 / HEAD_DIM)))
    ang = pos.astype(F32)[:, None] * inv[None, :]
    cos, sin = jnp.cos(ang), jnp.sin(ang)
    reps = LANES // HEAD_DIM
    return (jnp.tile(jnp.concatenate([cos, cos], axis=1), (1, reps)),
            jnp.tile(jnp.concatenate([-sin, sin], axis=1), (1, reps)))


def _pair_heads_cols(w_q):
    d = w_q.shape[0]
    per = A_Q_HEADS // A_KV_HEADS
    return w_q.reshape(d, A_KV_HEADS, per, HEAD_DIM).transpose(0, 2, 1, 3).reshape(d, A_Q_W)


def kernel(x_prompt, x_sample, cache_a_k, cache_a_v, cache_b_k, cache_b_v, state_conv, g_mix_pre, w_in, sinks,
           rel_bias, w_oa, w_ob, w_out, g_mix_post, g_ffn_pre, w_up, conv_w, conv_b, w_down, g_ffn_post):
    depth = w_in.shape[0]
    n_p, t_p, d_model = x_prompt.shape
    n_s, t_s, _ = x_sample.shape
    d_ff = w_down.shape[1]
    per = A_Q_HEADS // A_KV_HEADS
    tm = 512

    cos_p, sin_p = _rope_tables(jnp.arange(t_p))
    cos_s, sin_s = _rope_tables(jnp.tile(PAST_LEN + jnp.arange(t_s), n_s))

    yp = x_prompt.reshape(n_p * t_p, d_model)
    ys = x_sample.reshape(n_s * t_s, d_model)
    new_p, new_s = [], []
    for l in range(depth):
        w_in_bf = jnp.concatenate([_pair_heads_cols(w_in[l][:, :A_Q_W]), w_in[l][:, A_Q_W:]], axis=1).astype(BF16)
        w = dict(
            w_oa=w_oa[l].reshape(A_KV_HEADS, per, HEAD_DIM, d_model).transpose(1, 0, 2, 3)
                        .reshape(A_Q_W, d_model).astype(BF16),
            w_ob=w_ob[l].astype(BF16), w_out=w_out[l].astype(BF16), g_mix_post=g_mix_post[l],
            g_ffn_pre=g_ffn_pre[l], w_up=_interleave_up(w_up[l], d_ff).astype(BF16), conv_w=conv_w[l],
            conv_b=conv_b[l],
            w_down=w_down[l].astype(BF16), g_ffn_post=g_ffn_post[l])
        sink_heads = (sinks[l].astype(F32) * LOG2E).reshape(A_KV_HEADS, per).T.reshape(N_PAIRS, 2)

        def sink_rows(n):
            return jnp.broadcast_to(jnp.repeat(sink_heads, n, axis=1)[:, :, None], (N_PAIRS, 2 * n, LANES))

        bias = _build_bias(rel_bias[l])
        bias_p = bias.reshape(N_PAIRS, 2 * CHUNK, B_BAND_PAD)
        keys_s = cache_b_k.shape[2] + t_s
        bias_s = bias[:, :t_s, :keys_s].reshape(N_PAIRS, 2 * t_s, keys_s)

        keep_a, keep_b = min(A_WINDOW, t_p), min(B_REACH, t_p)
        (qa, ka, va, qb, kb, vb, sga, sgb, ak, av, bk, bv) = _in_projection(
            yp, g_mix_pre[l], w_in_bf, cos_p, sin_p, n_seq=n_p, seq_rows=t_p, tm=2 * tm,
            keep_a=keep_a, keep_b=keep_b, name="prompt_in_projection")
        oa, ob = _prompt_attention(qa, ka, va, qb, kb, vb, bias_p, sink_rows(CHUNK),
                                   n_seq=n_p, seq_rows=t_p, tq=min(2 * tm, t_p))
        yp, conv_p = _post(yp, oa, ob, sga, sgb, w, n_seq=n_p, seq_rows=t_p, tm=tm, name="prompt_post")
        new_p.append((ak.reshape(n_p, keep_a, A_KV_HEADS, HEAD_DIM), av.reshape(n_p, keep_a, A_KV_HEADS, HEAD_DIM),
                      bk.reshape(n_p, keep_b, B_HEADS, HEAD_DIM), bv.reshape(n_p, keep_b, B_HEADS, HEAD_DIM),
                      conv_p))

        rows_s = n_s * t_s
        (qa, ka, va, qb, kb, vb, sga, sgb, ak, av, bk, bv) = _in_projection(
            ys, g_mix_pre[l], w_in_bf, cos_s, sin_s, n_seq=1, seq_rows=rows_s, tm=rows_s,
            keep_a=rows_s, keep_b=rows_s, name="sample_in_projection")
        flat_heads = lambda c: c.reshape(c.shape[0], c.shape[1], -1).astype(BF16)
        oa, ob = _sample_attention(qa, ka, va, flat_heads(cache_a_k[l]), flat_heads(cache_a_v[l]),
                                   qb, kb, vb, flat_heads(cache_b_k[l]), flat_heads(cache_b_v[l]),
                                   bias_s, sink_rows(t_s), n_seq=n_s, seq_rows=t_s)
        st = state_conv[l].astype(F32)
        zeros = jnp.zeros((n_s, t_s - 2, d_ff), F32)
        prev1 = jnp.concatenate([st[:, 1:2], zeros, zeros[:, :1]], axis=1).reshape(rows_s, d_ff)
        prev2 = jnp.concatenate([st, zeros], axis=1).reshape(rows_s, d_ff)
        ys, u_s = _post(ys, oa, ob, sga, sgb, w, n_seq=n_s, seq_rows=t_s, tm=rows_s,
                        conv_bounds=(prev1, prev2), name="sample_post")
        new_s.append((ak.reshape(n_s, t_s, A_KV_HEADS, HEAD_DIM), av.reshape(n_s, t_s, A_KV_HEADS, HEAD_DIM),
                      bk.reshape(n_s, t_s, B_HEADS, HEAD_DIM), bv.reshape(n_s, t_s, B_HEADS, HEAD_DIM),
                      u_s.reshape(n_s, t_s, d_ff)[:, t_s - (CONV_WIDTH - 1):]))

    stack = lambda lst, k: jnp.stack([s[k] for s in lst], axis=0)
    return (yp.reshape(n_p, t_p, d_model), ys.reshape(n_s, t_s, d_model),
            stack(new_p, 0), stack(new_p, 1), stack(new_p, 2), stack(new_p, 3), stack(new_p, 4),
            stack(new_s, 0), stack(new_s, 1), stack(new_s, 2), stack(new_s, 3), stack(new_s, 4))
```

```python
import functools

import jax
import jax.numpy as jnp
from jax import lax
from jax.experimental import pallas as pl
from jax.experimental.pallas import tpu as pltpu

CHUNK = 64
HEAD_DIM = 64
A_Q_HEADS = 8
A_KV_HEADS = 2
A_WINDOW = 128
B_HEADS = 8
B_REACH = 512
REL_CLIP = 128
ROPE_THETA = 10000.0
CONV_WIDTH = 3
RMS_EPS = 1e-6
NEG_INF = -1e30
PAST_LEN = 1024

A_Q_W = A_Q_HEADS * HEAD_DIM
A_KV_W = A_KV_HEADS * HEAD_DIM
B_W = B_HEADS * HEAD_DIM
A_BAND = A_WINDOW + CHUNK
B_BAND = B_REACH + CHUNK
B_BAND_PAD = 640
LOG2E = 1.4426950408889634
LANES = 128
N_PAIRS = A_Q_W // LANES
TOEPLITZ_W = 1024
STEP_CHUNKS = 4
PART_ROWS = 256
FF_CHUNK = 2816
VMEM_LIMIT = 56 * 1024 * 1024

BF16 = jnp.bfloat16
F32 = jnp.float32


def _resident(shape):
    nd = len(shape)
    return pl.BlockSpec(shape, lambda *_: (0,) * nd, pipeline_mode=pl.Buffered(1))


def _rms(x, g):
    return (x * lax.rsqrt(jnp.mean(x * x, axis=-1, keepdims=True) + RMS_EPS)) * g


def _bias_kernel(t_ref, o_ref):
    row = lax.broadcasted_iota(jnp.int32, (CHUNK, TOEPLITZ_W), 0)
    col = lax.broadcasted_iota(jnp.int32, (CHUNK, B_BAND_PAD), 1)
    for head in range(t_ref.shape[0]):
        x = jnp.broadcast_to(t_ref[head], (CHUNK, TOEPLITZ_W))
        for b in range(CHUNK.bit_length() - 1):
            x = jnp.where((row >> b) & 1 == 1, pltpu.roll(x, 1 << b, axis=1), x)
        o_ref[head] = jnp.where(col < B_BAND, x[:, :B_BAND_PAD], NEG_INF)


def _build_bias(rel_table):
    h = rel_table.shape[0]
    tab = (rel_table.astype(F32) - rel_table[:, :1].astype(F32)) * LOG2E
    flat = B_REACH - REL_CLIP
    ramp = CHUNK + REL_CLIP
    row = jnp.concatenate([
        jnp.broadcast_to(tab[:, :1], (h, flat)),
        tab[:, :ramp],
        jnp.broadcast_to(tab[:, :1], (h, TOEPLITZ_W - flat - ramp)),
    ], axis=1).reshape(h, 1, TOEPLITZ_W)
    return pl.pallas_call(
        _bias_kernel,
        grid=(1,),
        in_specs=[pl.BlockSpec((h, 1, TOEPLITZ_W), lambda i: (0, 0, 0))],
        out_specs=pl.BlockSpec((h, CHUNK, B_BAND_PAD), lambda i: (0, 0, 0)),
        out_shape=jax.ShapeDtypeStruct((h, CHUNK, B_BAND_PAD), F32),
        name="rel_bias_tile",
    )(row)


def _rope(r, cos, sin_signed):
    lane = lax.broadcasted_iota(jnp.int32, (r.shape[0], LANES), 1)
    first_half = (lane & (HEAD_DIM // 2)) == 0
    out = []
    for j in range(r.shape[1] // LANES):
        blk = r[:, j * LANES:(j + 1) * LANES]
        up = pltpu.roll(blk, HEAD_DIM // 2, axis=1)
        down = pltpu.roll(blk, LANES - HEAD_DIM // 2, axis=1)
        out.append(blk * cos + jnp.where(first_half, down, up) * sin_signed)
    return out[0] if len(out) == 1 else jnp.concatenate(out, axis=1)


def _inproj_kernel(x_ref, g_ref, w_ref, cos_ref, sin_ref,
                   qa_ref, ka_ref, va_ref, qb_ref, kb_ref, vb_ref, sga_ref, sgb_ref,
                   ak_ref, av_ref, bk_ref, bv_ref, *, d_model, keep_a, keep_b, tm, n_split):
    i = pl.program_id(1)
    nt = pl.num_programs(1)
    scale = HEAD_DIM ** -0.5 * LOG2E
    part = tm // n_split

    def keep_tail(dst_ref, val, keep, r0):
        if keep <= tm:
            lo = max(r0, tm - keep)
            if lo < r0 + part:
                dst_ref[lo - (tm - keep):r0 + part - (tm - keep), :] = val[lo - r0:, :]
        else:
            first = nt - keep // tm

            @pl.when(i >= first)
            def _():
                dst_ref[pl.ds(pl.multiple_of((i - first) * tm + r0, part), part), :] = val

    for r0 in range(0, tm, part):
        rows = slice(r0, r0 + part)
        h = _rms(x_ref[rows, :], g_ref[...]).astype(BF16)
        r = jnp.dot(h, w_ref[...], preferred_element_type=F32)
        cos = cos_ref[rows, :]
        sin = sin_ref[rows, :]
        off = 0
        qa_ref[rows, :] = (_rope(r[:, off:off + A_Q_W], cos, sin) * scale).astype(BF16)
        off += A_Q_W
        ka = _rope(r[:, off:off + A_KV_W], cos, sin)
        ka_ref[rows, :] = ka.astype(BF16)
        keep_tail(ak_ref, ka, keep_a, r0)
        off += A_KV_W
        va = r[:, off:off + A_KV_W]
        va_ref[rows, :] = va.astype(BF16)
        keep_tail(av_ref, va, keep_a, r0)
        off += A_KV_W
        qb_ref[rows, :] = (r[:, off:off + B_W] * scale).astype(BF16)
        off += B_W
        kb = r[:, off:off + B_W]
        kb_ref[rows, :] = kb.astype(BF16)
        keep_tail(bk_ref, kb, keep_b, r0)
        off += B_W
        vb = r[:, off:off + B_W]
        vb_ref[rows, :] = vb.astype(BF16)
        keep_tail(bv_ref, vb, keep_b, r0)
        off += B_W
        sga_ref[rows, :] = jax.nn.sigmoid(r[:, off:off + d_model]).astype(BF16)
        off += d_model
        sgb_ref[rows, :] = jax.nn.sigmoid(r[:, off:off + d_model]).astype(BF16)


def _in_projection(x2d, g, w_in_bf, cos, sin, *, n_seq, seq_rows, tm, keep_a, keep_b, name):
    rows, d_model = x2d.shape
    nt = seq_rows // tm
    tok = lambda width: pl.BlockSpec((tm, width), lambda b, i: (b * nt + i, 0))
    tail = lambda keep, width: pl.BlockSpec((None, keep, width), lambda b, i: (b, 0, 0))
    tok_shape = lambda width: jax.ShapeDtypeStruct((rows, width), BF16)
    tail_shape = lambda keep, width: jax.ShapeDtypeStruct((n_seq, keep, width), F32)
    return pl.pallas_call(
        functools.partial(_inproj_kernel, d_model=d_model, keep_a=keep_a, keep_b=keep_b, tm=tm,
                          n_split=max(1, tm // PART_ROWS)),
        grid=(n_seq, nt),
        in_specs=[
            pl.BlockSpec((tm, d_model), lambda b, i: (b * nt + i, 0)),
            _resident((1, d_model)),
            _resident(w_in_bf.shape),
            pl.BlockSpec((tm, LANES), lambda b, i: (i, 0)),
            pl.BlockSpec((tm, LANES), lambda b, i: (i, 0)),
        ],
        out_specs=[tok(A_Q_W), tok(A_KV_W), tok(A_KV_W), tok(B_W), tok(B_W), tok(B_W),
                   tok(d_model), tok(d_model),
                   tail(keep_a, A_KV_W), tail(keep_a, A_KV_W), tail(keep_b, B_W), tail(keep_b, B_W)],
        out_shape=[tok_shape(A_Q_W), tok_shape(A_KV_W), tok_shape(A_KV_W), tok_shape(B_W), tok_shape(B_W),
                   tok_shape(B_W), tok_shape(d_model), tok_shape(d_model),
                   tail_shape(keep_a, A_KV_W), tail_shape(keep_a, A_KV_W),
                   tail_shape(keep_b, B_W), tail_shape(keep_b, B_W)],
        compiler_params=pltpu.CompilerParams(
            dimension_semantics=("arbitrary", "arbitrary"), vmem_limit_bytes=VMEM_LIMIT),
        name=name,
    )(x2d, g.reshape(1, d_model).astype(F32), w_in_bf, cos, sin)


def _stack_pair(q):
    lo = lax.broadcasted_iota(jnp.int32, q.shape, 1) < HEAD_DIM
    zero = jnp.zeros_like(q)
    return jnp.concatenate([jnp.where(lo, q, zero), jnp.where(lo, zero, q)], axis=0)


def _unstack_pair(o):
    n = o.shape[0] // 2
    lo = lax.broadcasted_iota(jnp.int32, (n, LANES), 1) < HEAD_DIM
    return jnp.where(lo, o[:n], o[n:])


def _scores(q2, k, *, bias=None, valid=None):
    s = lax.dot_general(q2, k, (((1,), (1,)), ((), ())), preferred_element_type=F32)
    if bias is not None:
        lead = s.shape[1] - bias.shape[1]
        s = s + bias if lead == 0 else jnp.concatenate([s[:, :lead], s[:, lead:] + bias], axis=1)
    if valid is not None:
        s = jnp.where(valid, s, NEG_INF)
    return s


def _row_max(s, sink=None):
    m = jnp.broadcast_to(jnp.max(s, axis=-1, keepdims=True), (s.shape[0], LANES))
    return m if sink is None else jnp.maximum(m, sink)


def _softmax_terms(s, m, *, sink=None):
    nk = s.shape[1]
    p = jnp.concatenate([jnp.exp2(s[:, k:min(k + LANES, nk)] - m[:, :min(LANES, nk - k)])
                         for k in range(0, nk, LANES)], axis=1)
    denom = jnp.broadcast_to(jnp.sum(p, axis=-1, keepdims=True), m.shape)
    if sink is not None:
        denom = denom + jnp.exp2(sink - m)
    return p.astype(BF16), 1.0 / denom


def _softmax_pv(s, m, v, *, sink=None):
    p, recip = _softmax_terms(s, m, sink=sink)
    return jnp.dot(p, v, preferred_element_type=F32) * recip


def _pair_attention(q, k, v, *, bias=None, sink=None):
    s = _scores(_stack_pair(q), k, bias=bias)
    return _unstack_pair(_softmax_pv(s, _row_max(s, sink), v, sink=sink))


def _prompt_attn_kernel(qa_ref, ka_ref, va_ref, qb_ref, kb_ref, vb_ref, bias_ref, sink_ref,
                        oa_ref, ob_ref, kap_all, vap_all, kbp_all, vbp_all, kv_sem,
                        sa_scr, ma_scr, sb_scr, mb_scr, *, seq_rows, tq):
    b = pl.program_id(0)
    i = pl.program_id(1)
    n_seq = pl.num_programs(0)
    slot = b % 2

    def kv_copies(seq, dst_slot):
        pairs = [(ka_ref, kap_all, A_WINDOW), (va_ref, vap_all, A_WINDOW),
                 (kb_ref, kbp_all, B_REACH), (vb_ref, vbp_all, B_REACH)]
        return [pltpu.make_async_copy(src.at[seq], dst.at[dst_slot, pl.ds(pad, seq_rows), :],
                                      kv_sem.at[dst_slot, n])
                for n, (src, dst, pad) in enumerate(pairs)]

    @pl.when(i == 0)
    def _():
        @pl.when(b == 0)
        def _():
            for s in range(2):
                kap_all[s, 0:A_WINDOW, :] = jnp.zeros((A_WINDOW, LANES), BF16)
                vap_all[s, 0:A_WINDOW, :] = jnp.zeros((A_WINDOW, LANES), BF16)
                kbp_all[s, 0:B_REACH, :] = jnp.zeros((B_REACH, B_W), BF16)
                vbp_all[s, 0:B_REACH, :] = jnp.zeros((B_REACH, B_W), BF16)
                kbp_all[s, B_REACH + seq_rows:, :] = jnp.zeros((B_BAND_PAD - B_BAND, B_W), BF16)
                vbp_all[s, B_REACH + seq_rows:, :] = jnp.zeros((B_BAND_PAD - B_BAND, B_W), BF16)
            for copy in kv_copies(0, 0):
                copy.start()

        for copy in kv_copies(b, slot):
            copy.wait()

        @pl.when(b + 1 < n_seq)
        def _():
            for copy in kv_copies(b + 1, 1 - slot):
                copy.start()

    kap, vap, kbp, vbp = kap_all.at[slot], vap_all.at[slot], kbp_all.at[slot], vbp_all.at[slot]

    col_a = lax.broadcasted_iota(jnp.int32, (2 * CHUNK, A_BAND), 1)
    col_b = lax.broadcasted_iota(jnp.int32, (2 * CHUNK, B_BAND_PAD), 1)

    n_chunks = tq // CHUNK

    def band_start(cc):
        start = (i * n_chunks + cc) * CHUNK
        return start, pl.multiple_of(cc * CHUNK, CHUNK), pl.multiple_of(start, CHUNK)

    def score_stage(cc, buf, masked):
        start, r0, w0 = band_start(cc)
        valid_a = (col_a >= A_WINDOW - start) if masked else None
        valid_b = (col_b >= B_REACH - start) if masked else None
        for j in range(N_PAIRS):
            sl = slice(j * LANES, (j + 1) * LANES)
            s = _scores(_stack_pair(qa_ref[pl.ds(r0, CHUNK), sl]), kap[pl.ds(w0, A_BAND), :], valid=valid_a)
            sa_scr[buf, j] = s
            ma_scr[buf, j] = _row_max(s, sink_ref[j])
        for j in range(N_PAIRS):
            sl = slice(j * LANES, (j + 1) * LANES)
            s = _scores(_stack_pair(qb_ref[pl.ds(r0, CHUNK), sl]), kbp[pl.ds(w0, B_BAND_PAD), sl],
                        bias=bias_ref[j], valid=valid_b)
            sb_scr[buf, j] = s
            mb_scr[buf, j] = _row_max(s)

    def output_stage(cc, buf):
        _, r0, w0 = band_start(cc)
        for j in range(N_PAIRS):
            sl = slice(j * LANES, (j + 1) * LANES)
            o = _softmax_pv(sa_scr[buf, j], ma_scr[buf, j], vap[pl.ds(w0, A_BAND), :], sink=sink_ref[j])
            oa_ref[pl.ds(r0, CHUNK), sl] = _unstack_pair(o.astype(BF16))
        for j in range(N_PAIRS):
            sl = slice(j * LANES, (j + 1) * LANES)
            o = _softmax_pv(sb_scr[buf, j], mb_scr[buf, j], vbp[pl.ds(w0, B_BAND_PAD), sl])
            ob_ref[pl.ds(r0, CHUNK), sl] = _unstack_pair(o.astype(BF16))

    def run(masked):
        k = STEP_CHUNKS
        n_steps = n_chunks // k
        for c in range(k):
            score_stage(c, c, masked)

        def step(t, carry):
            for parity in (0, 1):
                @pl.when(t % 2 == parity)
                def _():
                    for c in range(k):
                        output_stage(k * t + c, k * parity + c)
                    for c in range(k):
                        score_stage(k * (t + 1) + c, k * (1 - parity) + c, masked)
            return carry

        lax.fori_loop(0, n_steps - 1, step, 0)
        for c in range(k):
            output_stage(n_chunks - k + c, k * ((n_steps - 1) % 2) + c)

    @pl.when(i == 0)
    def _():
        run(True)

    @pl.when(i != 0)
    def _():
        run(False)


def _prompt_attention(qa, ka, va, qb, kb, vb, bias_pairs, sink_pairs, *, n_seq, seq_rows, tq):
    rows = n_seq * seq_rows
    nt = seq_rows // tq
    qspec = lambda width: pl.BlockSpec((tq, width), lambda b, i: (b * nt + i, 0))
    kvspec = pl.BlockSpec(memory_space=pl.ANY)
    seq3 = lambda a: a.reshape(n_seq, seq_rows, a.shape[-1])
    return pl.pallas_call(
        functools.partial(_prompt_attn_kernel, seq_rows=seq_rows, tq=tq),
        grid=(n_seq, nt),
        in_specs=[qspec(A_Q_W), kvspec, kvspec, qspec(B_W), kvspec, kvspec,
                  _resident(bias_pairs.shape), _resident(sink_pairs.shape)],
        out_specs=[qspec(A_Q_W), qspec(B_W)],
        out_shape=[jax.ShapeDtypeStruct((rows, A_Q_W), BF16), jax.ShapeDtypeStruct((rows, B_W), BF16)],
        scratch_shapes=[pltpu.VMEM((2, A_WINDOW + seq_rows, A_KV_W), BF16),
                        pltpu.VMEM((2, A_WINDOW + seq_rows, A_KV_W), BF16),
                        pltpu.VMEM((2, B_REACH + seq_rows + B_BAND_PAD - B_BAND, B_W), BF16),
                        pltpu.VMEM((2, B_REACH + seq_rows + B_BAND_PAD - B_BAND, B_W), BF16),
                        pltpu.SemaphoreType.DMA((2, 4)),
                        pltpu.VMEM((2 * STEP_CHUNKS, N_PAIRS, 2 * CHUNK, A_BAND), F32),
                        pltpu.VMEM((2 * STEP_CHUNKS, N_PAIRS, 2 * CHUNK, LANES), F32),
                        pltpu.VMEM((2 * STEP_CHUNKS, N_PAIRS, 2 * CHUNK, B_BAND_PAD), F32),
                        pltpu.VMEM((2 * STEP_CHUNKS, N_PAIRS, 2 * CHUNK, LANES), F32)],
        compiler_params=pltpu.CompilerParams(
            dimension_semantics=("arbitrary", "arbitrary"), vmem_limit_bytes=VMEM_LIMIT),
        name="prompt_attention",
    )(qa, seq3(ka), seq3(va), qb, seq3(kb), seq3(vb), bias_pairs, sink_pairs)


def _sample_attn_kernel(qa_ref, ka_ref, va_ref, cak_ref, cav_ref, qb_ref, kb_ref, vb_ref, cbk_ref, cbv_ref,
                        bias_ref, sink_ref, oa_ref, ob_ref):
    ka = jnp.concatenate([cak_ref[...], ka_ref[...]], axis=0)
    va = jnp.concatenate([cav_ref[...], va_ref[...]], axis=0)
    for j in range(N_PAIRS):
        sl = slice(j * LANES, (j + 1) * LANES)
        oa_ref[:, sl] = _pair_attention(qa_ref[:, sl], ka, va, sink=sink_ref[j]).astype(BF16)
    for j in range(N_PAIRS):
        sl = slice(j * LANES, (j + 1) * LANES)
        kb = jnp.concatenate([cbk_ref[:, sl], kb_ref[:, sl]], axis=0)
        vb = jnp.concatenate([cbv_ref[:, sl], vb_ref[:, sl]], axis=0)
        ob_ref[:, sl] = _pair_attention(qb_ref[:, sl], kb, vb, bias=bias_ref[j]).astype(BF16)


def _sample_attention(qa, ka, va, cak, cav, qb, kb, vb, cbk, cbv, bias_pairs, sink_pairs, *, n_seq, seq_rows):
    rows = n_seq * seq_rows
    new = lambda width: pl.BlockSpec((seq_rows, width), lambda b: (b, 0))
    past = lambda a: pl.BlockSpec((None,) + a.shape[1:], lambda b: (b, 0, 0))
    return pl.pallas_call(
        _sample_attn_kernel,
        grid=(n_seq,),
        in_specs=[new(A_Q_W), new(A_KV_W), new(A_KV_W), past(cak), past(cav),
                  new(B_W), new(B_W), new(B_W), past(cbk), past(cbv),
                  _resident(bias_pairs.shape), _resident(sink_pairs.shape)],
        out_specs=[new(A_Q_W), new(B_W)],
        out_shape=[jax.ShapeDtypeStruct((rows, A_Q_W), BF16), jax.ShapeDtypeStruct((rows, B_W), BF16)],
        compiler_params=pltpu.CompilerParams(dimension_semantics=("arbitrary",)),
        name="sample_attention",
    )(qa, ka, va, cak, cav, qb, kb, vb, cbk, cbv, bias_pairs, sink_pairs)


def _ff_chunks(d_ff):
    return [(lo, min(FF_CHUNK, d_ff - lo)) for lo in range(0, d_ff, FF_CHUNK)]


def _interleave_up(w_up, d_ff):
    return jnp.concatenate([w_up[:, base + lo:base + lo + width]
                            for lo, width in _ff_chunks(d_ff) for base in (0, d_ff)], axis=1)


def _post_kernel(*refs, tm, d_ff, seq_rows, carry_conv):
    if carry_conv:
        (x_ref, oa_ref, ob_ref, sga_ref, sgb_ref, woa_ref, wob_ref, wout_ref, gpost_ref, gpre_ref,
         wup_ref, cw_ref, cb_ref, wdown_ref, gffn_ref, y_ref, conv_ref, h2_ref) = refs
    else:
        (x_ref, oa_ref, ob_ref, sga_ref, sgb_ref, woa_ref, wob_ref, wout_ref, gpost_ref, gpre_ref,
         wup_ref, cw_ref, cb_ref, wdown_ref, gffn_ref, p1_ref, p2_ref, y_ref, u_ref, h2_ref) = refs
    i = pl.program_id(1)

    if carry_conv:
        @pl.when(i == 0)
        def _():
            conv_ref[...] = jnp.zeros_like(conv_ref)

    mixed = (sga_ref[...].astype(F32) * jnp.dot(oa_ref[...], woa_ref[...], preferred_element_type=F32)
             + sgb_ref[...].astype(F32) * jnp.dot(ob_ref[...], wob_ref[...], preferred_element_type=F32))
    z = jnp.dot(mixed.astype(BF16), wout_ref[...], preferred_element_type=F32)
    x1 = x_ref[...] + _rms(z, gpost_ref[...])
    y_ref[...] = x1
    h2_ref[...] = _rms(x1, gpre_ref[...]).astype(BF16)

    sqrt_half = 0.7071067811865476
    ff = None
    for lo, width in _ff_chunks(d_ff):
        cols = slice(lo, lo + width)
        uv = jnp.dot(h2_ref[...], wup_ref[:, 2 * lo:2 * (lo + width)], preferred_element_type=F32)
        u = uv[:, :width]
        val = uv[:, width:]
        row = lax.broadcasted_iota(jnp.int32, (tm, width), 0)
        back1 = pltpu.roll(u, 1, axis=0)
        back2 = pltpu.roll(u, 2, axis=0)
        if carry_conv:
            c0 = conv_ref[0:1, cols]
            c1 = conv_ref[1:2, cols]
            back1 = jnp.where(row == 0, c1, back1)
            back2 = jnp.where(row == 0, c0, jnp.where(row == 1, c1, back2))
            conv_ref[:, cols] = u[tm - (CONV_WIDTH - 1):, :]
        else:
            row_in_seq = row % seq_rows
            back1 = jnp.where(row_in_seq == 0, p1_ref[:, cols], back1)
            back2 = jnp.where(row_in_seq < 2, p2_ref[:, cols], back2)
            u_ref[:, cols] = u
        conv = cb_ref[:, cols] + back2 * cw_ref[0:1, cols]
        conv = conv + back1 * cw_ref[1:2, cols]
        conv = conv + u * cw_ref[2:3, cols]
        act = 0.5 * conv * (1.0 + lax.erf(conv * sqrt_half))
        part = jnp.dot((act * val).astype(BF16), wdown_ref[cols, :], preferred_element_type=F32)
        ff = part if ff is None else ff + part
    y_ref[...] = y_ref[...] + _rms(ff, gffn_ref[...])


def _post(x2d, oa, ob, sga, sgb, w, *, n_seq, seq_rows, tm, conv_bounds=None, name):
    rows, d_model = x2d.shape
    d_ff = w["w_down"].shape[0]
    carry_conv = conv_bounds is None
    nt = seq_rows // tm if carry_conv else 1
    grid = (n_seq, nt) if carry_conv else (1, 1)
    tok = lambda width: pl.BlockSpec((tm, width), lambda b, i: (b * nt + i, 0))
    vec = lambda a: a.reshape(1, -1).astype(F32)
    weights = [w["w_oa"], w["w_ob"], w["w_out"], vec(w["g_mix_post"]), vec(w["g_ffn_pre"]), w["w_up"],
               w["conv_w"].astype(F32), vec(w["conv_b"]), w["w_down"], vec(w["g_ffn_post"])]
    in_specs = [tok(d_model), tok(A_Q_W), tok(B_W), tok(d_model), tok(d_model)] + [_resident(a.shape) for a in weights]
    args = [x2d, oa, ob, sga, sgb] + weights
    if carry_conv:
        side_spec = pl.BlockSpec((None, CONV_WIDTH - 1, d_ff), lambda b, i: (b, 0, 0))
        side_shape = jax.ShapeDtypeStruct((n_seq, CONV_WIDTH - 1, d_ff), F32)
    else:
        in_specs += [tok(d_ff), tok(d_ff)]
        args += list(conv_bounds)
        side_spec = tok(d_ff)
        side_shape = jax.ShapeDtypeStruct((rows, d_ff), F32)
    return pl.pallas_call(
        functools.partial(_post_kernel, tm=tm, d_ff=d_ff, seq_rows=seq_rows, carry_conv=carry_conv),
        grid=grid,
        in_specs=in_specs,
        out_specs=[tok(d_model), side_spec],
        out_shape=[jax.ShapeDtypeStruct((rows, d_model), F32), side_shape],
        scratch_shapes=[pltpu.VMEM((tm, d_model), BF16)],
        compiler_params=pltpu.CompilerParams(
            dimension_semantics=("arbitrary", "arbitrary"), vmem_limit_bytes=VMEM_LIMIT),
        name=name,
    )(*args)


def _rope_tables(pos):
    half = HEAD_DIM // 2
    inv = 1.0 / (ROPE_THETA ** (jnp.arange(half, dtype=F32) * (2.0 / HEAD_DIM)))
    ang = pos.astype(F32)[:, None] * inv[None, :]
    cos, sin = jnp.cos(ang), jnp.sin(ang)
    reps = LANES // HEAD_DIM
    return (jnp.tile(jnp.concatenate([cos, cos], axis=1), (1, reps)),
            jnp.tile(jnp.concatenate([-sin, sin], axis=1), (1, reps)))


def _pair_heads_cols(w_q):
    d = w_q.shape[0]
    per = A_Q_HEADS // A_KV_HEADS
    return w_q.reshape(d, A_KV_HEADS, per, HEAD_DIM).transpose(0, 2, 1, 3).reshape(d, A_Q_W)


def kernel(x_prompt, x_sample, cache_a_k, cache_a_v, cache_b_k, cache_b_v, state_conv, g_mix_pre, w_in, sinks,
           rel_bias, w_oa, w_ob, w_out, g_mix_post, g_ffn_pre, w_up, conv_w, conv_b, w_down, g_ffn_post):
    depth = w_in.shape[0]
    n_p, t_p, d_model = x_prompt.shape
    n_s, t_s, _ = x_sample.shape
    d_ff = w_down.shape[1]
    per = A_Q_HEADS // A_KV_HEADS
    tm = 512

    cos_p, sin_p = _rope_tables(jnp.arange(t_p))
    cos_s, sin_s = _rope_tables(jnp.tile(PAST_LEN + jnp.arange(t_s), n_s))

    yp = x_prompt.reshape(n_p * t_p, d_model)
    ys = x_sample.reshape(n_s * t_s, d_model)
    new_p, new_s = [], []
    for l in range(depth):
        w_in_bf = jnp.concatenate([_pair_heads_cols(w_in[l][:, :A_Q_W]), w_in[l][:, A_Q_W:]], axis=1).astype(BF16)
        w = dict(
            w_oa=w_oa[l].reshape(A_KV_HEADS, per, HEAD_DIM, d_model).transpose(1, 0, 2, 3)
                        .reshape(A_Q_W, d_model).astype(BF16),
            w_ob=w_ob[l].astype(BF16), w_out=w_out[l].astype(BF16), g_mix_post=g_mix_post[l],
            g_ffn_pre=g_ffn_pre[l], w_up=_interleave_up(w_up[l], d_ff).astype(BF16), conv_w=conv_w[l],
            conv_b=conv_b[l],
            w_down=w_down[l].astype(BF16), g_ffn_post=g_ffn_post[l])
        sink_heads = (sinks[l].astype(F32) * LOG2E).reshape(A_KV_HEADS, per).T.reshape(N_PAIRS, 2)

        def sink_rows(n):
            return jnp.broadcast_to(jnp.repeat(sink_heads, n, axis=1)[:, :, None], (N_PAIRS, 2 * n, LANES))

        bias = _build_bias(rel_bias[l])
        far = (B_REACH - REL_CLIP) // LANES * LANES
        bias_p = bias[:, :, far:].reshape(N_PAIRS, 2 * CHUNK, B_BAND_PAD - far)
        keys_s = cache_b_k.shape[2] + t_s
        bias_s = bias[:, :t_s, :keys_s].reshape(N_PAIRS, 2 * t_s, keys_s)

        keep_a, keep_b = min(A_WINDOW, t_p), min(B_REACH, t_p)
        (qa, ka, va, qb, kb, vb, sga, sgb, ak, av, bk, bv) = _in_projection(
            yp, g_mix_pre[l], w_in_bf, cos_p, sin_p, n_seq=n_p, seq_rows=t_p, tm=2 * tm,
            keep_a=keep_a, keep_b=keep_b, name="prompt_in_projection")
        oa, ob = _prompt_attention(qa, ka, va, qb, kb, vb, bias_p, sink_rows(CHUNK),
                                   n_seq=n_p, seq_rows=t_p, tq=min(2 * tm, t_p))
        yp, conv_p = _post(yp, oa, ob, sga, sgb, w, n_seq=n_p, seq_rows=t_p, tm=tm, name="prompt_post")
        new_p.append((ak.reshape(n_p, keep_a, A_KV_HEADS, HEAD_DIM), av.reshape(n_p, keep_a, A_KV_HEADS, HEAD_DIM),
                      bk.reshape(n_p, keep_b, B_HEADS, HEAD_DIM), bv.reshape(n_p, keep_b, B_HEADS, HEAD_DIM),
                      conv_p))

        rows_s = n_s * t_s
        (qa, ka, va, qb, kb, vb, sga, sgb, ak, av, bk, bv) = _in_projection(
            ys, g_mix_pre[l], w_in_bf, cos_s, sin_s, n_seq=1, seq_rows=rows_s, tm=rows_s,
            keep_a=rows_s, keep_b=rows_s, name="sample_in_projection")
        flat_heads = lambda c: c.reshape(c.shape[0], c.shape[1], -1).astype(BF16)
        oa, ob = _sample_attention(qa, ka, va, flat_heads(cache_a_k[l]), flat_heads(cache_a_v[l]),
                                   qb, kb, vb, flat_heads(cache_b_k[l]), flat_heads(cache_b_v[l]),
                                   bias_s, sink_rows(t_s), n_seq=n_s, seq_rows=t_s)
        st = state_conv[l].astype(F32)
        zeros = jnp.zeros((n_s, t_s - 2, d_ff), F32)
        prev1 = jnp.concatenate([st[:, 1:2], zeros, zeros[:, :1]], axis=1).reshape(rows_s, d_ff)
        prev2 = jnp.concatenate([st, zeros], axis=1).reshape(rows_s, d_ff)
        ys, u_s = _post(ys, oa, ob, sga, sgb, w, n_seq=n_s, seq_rows=t_s, tm=rows_s,
                        conv_bounds=(prev1, prev2), name="sample_post")
        new_s.append((ak.reshape(n_s, t_s, A_KV_HEADS, HEAD_DIM), av.reshape(n_s, t_s, A_KV_HEADS, HEAD_DIM),
                      bk.reshape(n_s, t_s, B_HEADS, HEAD_DIM), bv.reshape(n_s, t_s, B_HEADS, HEAD_DIM),
                      u_s.reshape(n_s, t_s, d_ff)[:, t_s - (CONV_WIDTH - 1):]))

    stack = lambda lst, k: jnp.stack([s[k] for s in lst], axis=0)
    return (yp.reshape(n_p, t_p, d_model), ys.reshape(n_s, t_s, d_model),
            stack(new_p, 0), stack(new_p, 1), stack(new_p, 2), stack(new_p, 3), stack(new_p, 4),
            stack(new_s, 0), stack(new_s, 1), stack(new_s, 2), stack(new_s, 3), stack(new_s, 4))
```

```python
import functools

import jax
import jax.numpy as jnp
from jax import lax
from jax.experimental import pallas as pl
from jax.experimental.pallas import tpu as pltpu

CHUNK = 64
HEAD_DIM = 64
A_Q_HEADS = 8
A_KV_HEADS = 2
A_WINDOW = 128
B_HEADS = 8
B_REACH = 512
REL_CLIP = 128
ROPE_THETA = 10000.0
CONV_WIDTH = 3
RMS_EPS = 1e-6
NEG_INF = -1e30
PAST_LEN = 1024

A_Q_W = A_Q_HEADS * HEAD_DIM
A_KV_W = A_KV_HEADS * HEAD_DIM
B_W = B_HEADS * HEAD_DIM
LANES = 128
N_PAIRS = A_Q_W // LANES
A_BAND = A_WINDOW + CHUNK
B_BAND = B_REACH + CHUNK
B_BAND_PAD = -(-B_BAND // LANES) * LANES
TOEPLITZ_W = 1 << (B_BAND_PAD + CHUNK - 1).bit_length()
LOG2E = 1.4426950408889634
ROW_TILE = 512
IN_PROJ_TILE = 2 * ROW_TILE
PART_ROWS = 256
ATTN_TILE = 2 * ROW_TILE
STEP_CHUNKS = 4
FF_CHUNK = 2816
VMEM_LIMIT = 56 * 1024 * 1024

BF16 = jnp.bfloat16
F32 = jnp.float32


def _resident(shape):
    nd = len(shape)
    return pl.BlockSpec(shape, lambda *_: (0,) * nd, pipeline_mode=pl.Buffered(1))


def _rms(x, g):
    return (x * lax.rsqrt(jnp.mean(x * x, axis=-1, keepdims=True) + RMS_EPS)) * g


def _bias_kernel(t_ref, o_ref):
    row = lax.broadcasted_iota(jnp.int32, (CHUNK, TOEPLITZ_W), 0)
    col = lax.broadcasted_iota(jnp.int32, (CHUNK, B_BAND_PAD), 1)
    for head in range(t_ref.shape[0]):
        x = jnp.broadcast_to(t_ref[head], (CHUNK, TOEPLITZ_W))
        for b in range(CHUNK.bit_length() - 1):
            x = jnp.where((row >> b) & 1 == 1, pltpu.roll(x, 1 << b, axis=1), x)
        o_ref[head] = jnp.where(col < B_BAND, x[:, :B_BAND_PAD], NEG_INF)


def _build_bias(rel_table):
    h = rel_table.shape[0]
    tab = (rel_table.astype(F32) - rel_table[:, :1].astype(F32)) * LOG2E
    flat = B_REACH - REL_CLIP
    ramp = CHUNK + REL_CLIP
    row = jnp.concatenate([
        jnp.broadcast_to(tab[:, :1], (h, flat)),
        tab[:, :ramp],
        jnp.broadcast_to(tab[:, :1], (h, TOEPLITZ_W - flat - ramp)),
    ], axis=1).reshape(h, 1, TOEPLITZ_W)
    return pl.pallas_call(
        _bias_kernel,
        grid=(1,),
        in_specs=[pl.BlockSpec((h, 1, TOEPLITZ_W), lambda i: (0, 0, 0))],
        out_specs=pl.BlockSpec((h, CHUNK, B_BAND_PAD), lambda i: (0, 0, 0)),
        out_shape=jax.ShapeDtypeStruct((h, CHUNK, B_BAND_PAD), F32),
        name="rel_bias_tile",
    )(row)


def _rope(r, cos, sin_signed):
    lane = lax.broadcasted_iota(jnp.int32, (r.shape[0], LANES), 1)
    first_half = (lane & (HEAD_DIM // 2)) == 0
    out = []
    for j in range(r.shape[1] // LANES):
        blk = r[:, j * LANES:(j + 1) * LANES]
        up = pltpu.roll(blk, HEAD_DIM // 2, axis=1)
        down = pltpu.roll(blk, LANES - HEAD_DIM // 2, axis=1)
        out.append(blk * cos + jnp.where(first_half, down, up) * sin_signed)
    return out[0] if len(out) == 1 else jnp.concatenate(out, axis=1)


def _inproj_kernel(x_ref, g_ref, w_ref, cos_ref, sin_ref,
                   qa_ref, ka_ref, va_ref, qb_ref, kb_ref, vb_ref, sga_ref, sgb_ref,
                   ak_ref, av_ref, bk_ref, bv_ref, *, d_model, keep_a, keep_b, tm, n_split):
    i = pl.program_id(1)
    nt = pl.num_programs(1)
    scale = HEAD_DIM ** -0.5 * LOG2E
    part = tm // n_split

    def keep_tail(dst_ref, val, keep, r0):
        if keep <= tm:
            lo = max(r0, tm - keep)
            if lo < r0 + part:
                dst_ref[lo - (tm - keep):r0 + part - (tm - keep), :] = val[lo - r0:, :]
        else:
            first = nt - keep // tm

            @pl.when(i >= first)
            def _():
                dst_ref[pl.ds(pl.multiple_of((i - first) * tm + r0, part), part), :] = val

    for r0 in range(0, tm, part):
        rows = slice(r0, r0 + part)
        h = _rms(x_ref[rows, :], g_ref[...]).astype(BF16)
        r = jnp.dot(h, w_ref[...], preferred_element_type=F32)
        cos = cos_ref[rows, :]
        sin = sin_ref[rows, :]
        off = 0
        qa_ref[rows, :] = (_rope(r[:, off:off + A_Q_W], cos, sin) * scale).astype(BF16)
        off += A_Q_W
        ka = _rope(r[:, off:off + A_KV_W], cos, sin)
        ka_ref[rows, :] = ka.astype(BF16)
        keep_tail(ak_ref, ka, keep_a, r0)
        off += A_KV_W
        va = r[:, off:off + A_KV_W]
        va_ref[rows, :] = va.astype(BF16)
        keep_tail(av_ref, va, keep_a, r0)
        off += A_KV_W
        qb_ref[rows, :] = (r[:, off:off + B_W] * scale).astype(BF16)
        off += B_W
        kb = r[:, off:off + B_W]
        kb_ref[rows, :] = kb.astype(BF16)
        keep_tail(bk_ref, kb, keep_b, r0)
        off += B_W
        vb = r[:, off:off + B_W]
        vb_ref[rows, :] = vb.astype(BF16)
        keep_tail(bv_ref, vb, keep_b, r0)
        off += B_W
        sga_ref[rows, :] = jax.nn.sigmoid(r[:, off:off + d_model]).astype(BF16)
        off += d_model
        sgb_ref[rows, :] = jax.nn.sigmoid(r[:, off:off + d_model]).astype(BF16)


def _in_projection(x2d, g, w_in_bf, cos, sin, *, n_seq, seq_rows, tm, keep_a, keep_b, name):
    rows, d_model = x2d.shape
    nt = seq_rows // tm
    tok = lambda width: pl.BlockSpec((tm, width), lambda b, i: (b * nt + i, 0))
    tail = lambda keep, width: pl.BlockSpec((None, keep, width), lambda b, i: (b, 0, 0))
    tok_shape = lambda width: jax.ShapeDtypeStruct((rows, width), BF16)
    tail_shape = lambda keep, width: jax.ShapeDtypeStruct((n_seq, keep, width), F32)
    return pl.pallas_call(
        functools.partial(_inproj_kernel, d_model=d_model, keep_a=keep_a, keep_b=keep_b, tm=tm,
                          n_split=max(1, tm // PART_ROWS)),
        grid=(n_seq, nt),
        in_specs=[
            pl.BlockSpec((tm, d_model), lambda b, i: (b * nt + i, 0)),
            _resident((1, d_model)),
            _resident(w_in_bf.shape),
            pl.BlockSpec((tm, LANES), lambda b, i: (i, 0)),
            pl.BlockSpec((tm, LANES), lambda b, i: (i, 0)),
        ],
        out_specs=[tok(A_Q_W), tok(A_KV_W), tok(A_KV_W), tok(B_W), tok(B_W), tok(B_W),
                   tok(d_model), tok(d_model),
                   tail(keep_a, A_KV_W), tail(keep_a, A_KV_W), tail(keep_b, B_W), tail(keep_b, B_W)],
        out_shape=[tok_shape(A_Q_W), tok_shape(A_KV_W), tok_shape(A_KV_W), tok_shape(B_W), tok_shape(B_W),
                   tok_shape(B_W), tok_shape(d_model), tok_shape(d_model),
                   tail_shape(keep_a, A_KV_W), tail_shape(keep_a, A_KV_W),
                   tail_shape(keep_b, B_W), tail_shape(keep_b, B_W)],
        compiler_params=pltpu.CompilerParams(
            dimension_semantics=("arbitrary", "arbitrary"), vmem_limit_bytes=VMEM_LIMIT),
        name=name,
    )(x2d, g.reshape(1, d_model).astype(F32), w_in_bf, cos, sin)


def _stack_pair(q):
    lo = lax.broadcasted_iota(jnp.int32, q.shape, 1) < HEAD_DIM
    zero = jnp.zeros_like(q)
    return jnp.concatenate([jnp.where(lo, q, zero), jnp.where(lo, zero, q)], axis=0)


def _unstack_pair(o):
    n = o.shape[0] // 2
    lo = lax.broadcasted_iota(jnp.int32, (n, LANES), 1) < HEAD_DIM
    return jnp.where(lo, o[:n], o[n:])


def _scores(q2, k, *, bias=None, valid=None):
    s = lax.dot_general(q2, k, (((1,), (1,)), ((), ())), preferred_element_type=F32)
    if bias is not None:
        lead = s.shape[1] - bias.shape[1]
        s = s + bias if lead == 0 else jnp.concatenate([s[:, :lead], s[:, lead:] + bias], axis=1)
    if valid is not None:
        s = jnp.where(valid, s, NEG_INF)
    return s


def _row_max(s, sink=None):
    m = jnp.broadcast_to(jnp.max(s, axis=-1, keepdims=True), (s.shape[0], LANES))
    return m if sink is None else jnp.maximum(m, sink)


def _softmax_terms(s, m, *, sink=None):
    nk = s.shape[1]
    p = jnp.concatenate([jnp.exp2(s[:, k:min(k + LANES, nk)] - m[:, :min(LANES, nk - k)])
                         for k in range(0, nk, LANES)], axis=1)
    denom = jnp.broadcast_to(jnp.sum(p, axis=-1, keepdims=True), m.shape)
    if sink is not None:
        denom = denom + jnp.exp2(sink - m)
    return p.astype(BF16), 1.0 / denom


def _softmax_pv(s, m, v, *, sink=None):
    p, recip = _softmax_terms(s, m, sink=sink)
    return jnp.dot(p, v, preferred_element_type=F32) * recip


def _pair_attention(q, k, v, *, bias=None, sink=None):
    s = _scores(_stack_pair(q), k, bias=bias)
    return _unstack_pair(_softmax_pv(s, _row_max(s, sink), v, sink=sink))


def _prompt_attn_kernel(qa_ref, ka_ref, va_ref, qb_ref, kb_ref, vb_ref, bias_ref, sink_ref,
                        oa_ref, ob_ref, kap_all, vap_all, kbp_all, vbp_all, kv_sem,
                        sa_scr, ma_scr, sb_scr, mb_scr, *, seq_rows, tq):
    b = pl.program_id(0)
    i = pl.program_id(1)
    n_seq = pl.num_programs(0)
    slot = b % 2

    def kv_copies(seq, dst_slot):
        pairs = [(ka_ref, kap_all, A_WINDOW), (va_ref, vap_all, A_WINDOW),
                 (kb_ref, kbp_all, B_REACH), (vb_ref, vbp_all, B_REACH)]
        return [pltpu.make_async_copy(src.at[seq], dst.at[dst_slot, pl.ds(pad, seq_rows), :],
                                      kv_sem.at[dst_slot, n])
                for n, (src, dst, pad) in enumerate(pairs)]

    @pl.when(i == 0)
    def _():
        @pl.when(b == 0)
        def _():
            for s in range(2):
                kap_all[s, 0:A_WINDOW, :] = jnp.zeros((A_WINDOW, LANES), BF16)
                vap_all[s, 0:A_WINDOW, :] = jnp.zeros((A_WINDOW, LANES), BF16)
                kbp_all[s, 0:B_REACH, :] = jnp.zeros((B_REACH, B_W), BF16)
                vbp_all[s, 0:B_REACH, :] = jnp.zeros((B_REACH, B_W), BF16)
                kbp_all[s, B_REACH + seq_rows:, :] = jnp.zeros((B_BAND_PAD - B_BAND, B_W), BF16)
                vbp_all[s, B_REACH + seq_rows:, :] = jnp.zeros((B_BAND_PAD - B_BAND, B_W), BF16)
            for copy in kv_copies(0, 0):
                copy.start()

        for copy in kv_copies(b, slot):
            copy.wait()

        @pl.when(b + 1 < n_seq)
        def _():
            for copy in kv_copies(b + 1, 1 - slot):
                copy.start()

    kap, vap, kbp, vbp = kap_all.at[slot], vap_all.at[slot], kbp_all.at[slot], vbp_all.at[slot]

    col_a = lax.broadcasted_iota(jnp.int32, (2 * CHUNK, A_BAND), 1)
    col_b = lax.broadcasted_iota(jnp.int32, (2 * CHUNK, B_BAND_PAD), 1)

    n_chunks = tq // CHUNK

    def band_start(cc):
        start = (i * n_chunks + cc) * CHUNK
        return start, pl.multiple_of(cc * CHUNK, CHUNK), pl.multiple_of(start, CHUNK)

    def score_stage(cc, buf, masked):
        start, r0, w0 = band_start(cc)
        valid_a = (col_a >= A_WINDOW - start) if masked else None
        valid_b = (col_b >= B_REACH - start) if masked else None
        for j in range(N_PAIRS):
            sl = slice(j * LANES, (j + 1) * LANES)
            s = _scores(_stack_pair(qa_ref[pl.ds(r0, CHUNK), sl]), kap[pl.ds(w0, A_BAND), :], valid=valid_a)
            sa_scr[buf, j] = s
            ma_scr[buf, j] = _row_max(s, sink_ref[j])
        for j in range(N_PAIRS):
            sl = slice(j * LANES, (j + 1) * LANES)
            s = _scores(_stack_pair(qb_ref[pl.ds(r0, CHUNK), sl]), kbp[pl.ds(w0, B_BAND_PAD), sl],
                        bias=bias_ref[j], valid=valid_b)
            sb_scr[buf, j] = s
            mb_scr[buf, j] = _row_max(s)

    def output_stage(cc, buf):
        _, r0, w0 = band_start(cc)
        for j in range(N_PAIRS):
            sl = slice(j * LANES, (j + 1) * LANES)
            o = _softmax_pv(sa_scr[buf, j], ma_scr[buf, j], vap[pl.ds(w0, A_BAND), :], sink=sink_ref[j])
            oa_ref[pl.ds(r0, CHUNK), sl] = _unstack_pair(o.astype(BF16))
        for j in range(N_PAIRS):
            sl = slice(j * LANES, (j + 1) * LANES)
            o = _softmax_pv(sb_scr[buf, j], mb_scr[buf, j], vbp[pl.ds(w0, B_BAND_PAD), sl])
            ob_ref[pl.ds(r0, CHUNK), sl] = _unstack_pair(o.astype(BF16))

    def run(masked):
        k = STEP_CHUNKS
        n_steps = n_chunks // k
        for c in range(k):
            score_stage(c, c, masked)

        def step(t, carry):
            for parity in (0, 1):
                @pl.when(t % 2 == parity)
                def _():
                    for c in range(k):
                        output_stage(k * t + c, k * parity + c)
                    for c in range(k):
                        score_stage(k * (t + 1) + c, k * (1 - parity) + c, masked)
            return carry

        lax.fori_loop(0, n_steps - 1, step, 0)
        for c in range(k):
            output_stage(n_chunks - k + c, k * ((n_steps - 1) % 2) + c)

    @pl.when(i == 0)
    def _():
        run(True)

    @pl.when(i != 0)
    def _():
        run(False)


def _prompt_attention(qa, ka, va, qb, kb, vb, bias_pairs, sink_pairs, *, n_seq, seq_rows, tq):
    rows = n_seq * seq_rows
    nt = seq_rows // tq
    qspec = lambda width: pl.BlockSpec((tq, width), lambda b, i: (b * nt + i, 0))
    kvspec = pl.BlockSpec(memory_space=pl.ANY)
    seq3 = lambda a: a.reshape(n_seq, seq_rows, a.shape[-1])
    return pl.pallas_call(
        functools.partial(_prompt_attn_kernel, seq_rows=seq_rows, tq=tq),
        grid=(n_seq, nt),
        in_specs=[qspec(A_Q_W), kvspec, kvspec, qspec(B_W), kvspec, kvspec,
                  _resident(bias_pairs.shape), _resident(sink_pairs.shape)],
        out_specs=[qspec(A_Q_W), qspec(B_W)],
        out_shape=[jax.ShapeDtypeStruct((rows, A_Q_W), BF16), jax.ShapeDtypeStruct((rows, B_W), BF16)],
        scratch_shapes=[pltpu.VMEM((2, A_WINDOW + seq_rows, A_KV_W), BF16),
                        pltpu.VMEM((2, A_WINDOW + seq_rows, A_KV_W), BF16),
                        pltpu.VMEM((2, B_REACH + seq_rows + B_BAND_PAD - B_BAND, B_W), BF16),
                        pltpu.VMEM((2, B_REACH + seq_rows + B_BAND_PAD - B_BAND, B_W), BF16),
                        pltpu.SemaphoreType.DMA((2, 4)),
                        pltpu.VMEM((2 * STEP_CHUNKS, N_PAIRS, 2 * CHUNK, A_BAND), F32),
                        pltpu.VMEM((2 * STEP_CHUNKS, N_PAIRS, 2 * CHUNK, LANES), F32),
                        pltpu.VMEM((2 * STEP_CHUNKS, N_PAIRS, 2 * CHUNK, B_BAND_PAD), F32),
                        pltpu.VMEM((2 * STEP_CHUNKS, N_PAIRS, 2 * CHUNK, LANES), F32)],
        compiler_params=pltpu.CompilerParams(
            dimension_semantics=("arbitrary", "arbitrary"), vmem_limit_bytes=VMEM_LIMIT),
        name="prompt_attention",
    )(qa, seq3(ka), seq3(va), qb, seq3(kb), seq3(vb), bias_pairs, sink_pairs)


def _sample_attn_kernel(qa_ref, ka_ref, va_ref, cak_ref, cav_ref, qb_ref, kb_ref, vb_ref, cbk_ref, cbv_ref,
                        bias_ref, sink_ref, oa_ref, ob_ref):
    ka = jnp.concatenate([cak_ref[...], ka_ref[...]], axis=0)
    va = jnp.concatenate([cav_ref[...], va_ref[...]], axis=0)
    for j in range(N_PAIRS):
        sl = slice(j * LANES, (j + 1) * LANES)
        oa_ref[:, sl] = _pair_attention(qa_ref[:, sl], ka, va, sink=sink_ref[j]).astype(BF16)
    for j in range(N_PAIRS):
        sl = slice(j * LANES, (j + 1) * LANES)
        kb = jnp.concatenate([cbk_ref[:, sl], kb_ref[:, sl]], axis=0)
        vb = jnp.concatenate([cbv_ref[:, sl], vb_ref[:, sl]], axis=0)
        ob_ref[:, sl] = _pair_attention(qb_ref[:, sl], kb, vb, bias=bias_ref[j]).astype(BF16)


def _sample_attention(qa, ka, va, cak, cav, qb, kb, vb, cbk, cbv, bias_pairs, sink_pairs, *, n_seq, seq_rows):
    rows = n_seq * seq_rows
    new = lambda width: pl.BlockSpec((seq_rows, width), lambda b: (b, 0))
    past = lambda a: pl.BlockSpec((None,) + a.shape[1:], lambda b: (b, 0, 0))
    return pl.pallas_call(
        _sample_attn_kernel,
        grid=(n_seq,),
        in_specs=[new(A_Q_W), new(A_KV_W), new(A_KV_W), past(cak), past(cav),
                  new(B_W), new(B_W), new(B_W), past(cbk), past(cbv),
                  _resident(bias_pairs.shape), _resident(sink_pairs.shape)],
        out_specs=[new(A_Q_W), new(B_W)],
        out_shape=[jax.ShapeDtypeStruct((rows, A_Q_W), BF16), jax.ShapeDtypeStruct((rows, B_W), BF16)],
        compiler_params=pltpu.CompilerParams(dimension_semantics=("arbitrary",)),
        name="sample_attention",
    )(qa, ka, va, cak, cav, qb, kb, vb, cbk, cbv, bias_pairs, sink_pairs)


def _ff_chunks(d_ff):
    return [(lo, min(FF_CHUNK, d_ff - lo)) for lo in range(0, d_ff, FF_CHUNK)]


def _interleave_up(w_up, d_ff):
    return jnp.concatenate([w_up[:, base + lo:base + lo + width]
                            for lo, width in _ff_chunks(d_ff) for base in (0, d_ff)], axis=1)


def _post_kernel(*refs, tm, d_ff, seq_rows, carry_conv):
    if carry_conv:
        (x_ref, oa_ref, ob_ref, sga_ref, sgb_ref, woa_ref, wob_ref, wout_ref, gpost_ref, gpre_ref,
         wup_ref, cw_ref, cb_ref, wdown_ref, gffn_ref, y_ref, conv_ref, h2_ref) = refs
    else:
        (x_ref, oa_ref, ob_ref, sga_ref, sgb_ref, woa_ref, wob_ref, wout_ref, gpost_ref, gpre_ref,
         wup_ref, cw_ref, cb_ref, wdown_ref, gffn_ref, p1_ref, p2_ref, y_ref, u_ref, h2_ref) = refs
    i = pl.program_id(1)

    if carry_conv:
        @pl.when(i == 0)
        def _():
            conv_ref[...] = jnp.zeros_like(conv_ref)

    mixed = (sga_ref[...].astype(F32) * jnp.dot(oa_ref[...], woa_ref[...], preferred_element_type=F32)
             + sgb_ref[...].astype(F32) * jnp.dot(ob_ref[...], wob_ref[...], preferred_element_type=F32))
    z = jnp.dot(mixed.astype(BF16), wout_ref[...], preferred_element_type=F32)
    x1 = x_ref[...] + _rms(z, gpost_ref[...])
    y_ref[...] = x1
    h2_ref[...] = _rms(x1, gpre_ref[...]).astype(BF16)

    sqrt_half = 0.7071067811865476
    ff = None
    for lo, width in _ff_chunks(d_ff):
        cols = slice(lo, lo + width)
        uv = jnp.dot(h2_ref[...], wup_ref[:, 2 * lo:2 * (lo + width)], preferred_element_type=F32)
        u = uv[:, :width]
        val = uv[:, width:]
        row = lax.broadcasted_iota(jnp.int32, (tm, width), 0)
        back1 = pltpu.roll(u, 1, axis=0)
        back2 = pltpu.roll(u, 2, axis=0)
        if carry_conv:
            c0 = conv_ref[0:1, cols]
            c1 = conv_ref[1:2, cols]
            back1 = jnp.where(row == 0, c1, back1)
            back2 = jnp.where(row == 0, c0, jnp.where(row == 1, c1, back2))
            conv_ref[:, cols] = u[tm - (CONV_WIDTH - 1):, :]
        else:
            row_in_seq = row % seq_rows
            back1 = jnp.where(row_in_seq == 0, p1_ref[:, cols], back1)
            back2 = jnp.where(row_in_seq < 2, p2_ref[:, cols], back2)
            u_ref[:, cols] = u
        conv = cb_ref[:, cols] + back2 * cw_ref[0:1, cols]
        conv = conv + back1 * cw_ref[1:2, cols]
        conv = conv + u * cw_ref[2:3, cols]
        act = 0.5 * conv * (1.0 + lax.erf(conv * sqrt_half))
        part = jnp.dot((act * val).astype(BF16), wdown_ref[cols, :], preferred_element_type=F32)
        ff = part if ff is None else ff + part
    y_ref[...] = y_ref[...] + _rms(ff, gffn_ref[...])


def _post(x2d, oa, ob, sga, sgb, w, *, n_seq, seq_rows, tm, conv_bounds=None, name):
    rows, d_model = x2d.shape
    d_ff = w["w_down"].shape[0]
    carry_conv = conv_bounds is None
    nt = seq_rows // tm if carry_conv else 1
    grid = (n_seq, nt) if carry_conv else (1, 1)
    tok = lambda width: pl.BlockSpec((tm, width), lambda b, i: (b * nt + i, 0))
    vec = lambda a: a.reshape(1, -1).astype(F32)
    weights = [w["w_oa"], w["w_ob"], w["w_out"], vec(w["g_mix_post"]), vec(w["g_ffn_pre"]), w["w_up"],
               w["conv_w"].astype(F32), vec(w["conv_b"]), w["w_down"], vec(w["g_ffn_post"])]
    in_specs = [tok(d_model), tok(A_Q_W), tok(B_W), tok(d_model), tok(d_model)] + [_resident(a.shape) for a in weights]
    args = [x2d, oa, ob, sga, sgb] + weights
    if carry_conv:
        side_spec = pl.BlockSpec((None, CONV_WIDTH - 1, d_ff), lambda b, i: (b, 0, 0))
        side_shape = jax.ShapeDtypeStruct((n_seq, CONV_WIDTH - 1, d_ff), F32)
    else:
        in_specs += [tok(d_ff), tok(d_ff)]
        args += list(conv_bounds)
        side_spec = tok(d_ff)
        side_shape = jax.ShapeDtypeStruct((rows, d_ff), F32)
    return pl.pallas_call(
        functools.partial(_post_kernel, tm=tm, d_ff=d_ff, seq_rows=seq_rows, carry_conv=carry_conv),
        grid=grid,
        in_specs=in_specs,
        out_specs=[tok(d_model), side_spec],
        out_shape=[jax.ShapeDtypeStruct((rows, d_model), F32), side_shape],
        scratch_shapes=[pltpu.VMEM((tm, d_model), BF16)],
        compiler_params=pltpu.CompilerParams(
            dimension_semantics=("arbitrary", "arbitrary"), vmem_limit_bytes=VMEM_LIMIT),
        name=name,
    )(*args)


def _rope_tables(pos):
    half = HEAD_DIM // 2
    inv = 1.0 / (ROPE_THETA ** (jnp.arange(half, dtype=F32) * (2.0 / HEAD_DIM)))
    ang = pos.astype(F32)[:, None] * inv[None, :]
    cos, sin = jnp.cos(ang), jnp.sin(ang)
    reps = LANES // HEAD_DIM
    return (jnp.tile(jnp.concatenate([cos, cos], axis=1), (1, reps)),
            jnp.tile(jnp.concatenate([-sin, sin], axis=1), (1, reps)))


def _pair_heads_cols(w_q):
    d = w_q.shape[0]
    per = A_Q_HEADS // A_KV_HEADS
    return w_q.reshape(d, A_KV_HEADS, per, HEAD_DIM).transpose(0, 2, 1, 3).reshape(d, A_Q_W)


def kernel(x_prompt, x_sample, cache_a_k, cache_a_v, cache_b_k, cache_b_v, state_conv, g_mix_pre, w_in, sinks,
           rel_bias, w_oa, w_ob, w_out, g_mix_post, g_ffn_pre, w_up, conv_w, conv_b, w_down, g_ffn_post):
    depth = w_in.shape[0]
    n_p, t_p, d_model = x_prompt.shape
    n_s, t_s, _ = x_sample.shape
    d_ff = w_down.shape[1]
    per = A_Q_HEADS // A_KV_HEADS

    cos_p, sin_p = _rope_tables(jnp.arange(t_p))
    cos_s, sin_s = _rope_tables(jnp.tile(PAST_LEN + jnp.arange(t_s), n_s))

    yp = x_prompt.reshape(n_p * t_p, d_model)
    ys = x_sample.reshape(n_s * t_s, d_model)
    new_p, new_s = [], []
    for l in range(depth):
        w_in_bf = jnp.concatenate([_pair_heads_cols(w_in[l][:, :A_Q_W]), w_in[l][:, A_Q_W:]], axis=1).astype(BF16)
        w = dict(
            w_oa=w_oa[l].reshape(A_KV_HEADS, per, HEAD_DIM, d_model).transpose(1, 0, 2, 3)
                        .reshape(A_Q_W, d_model).astype(BF16),
            w_ob=w_ob[l].astype(BF16), w_out=w_out[l].astype(BF16), g_mix_post=g_mix_post[l],
            g_ffn_pre=g_ffn_pre[l], w_up=_interleave_up(w_up[l], d_ff).astype(BF16), conv_w=conv_w[l],
            conv_b=conv_b[l],
            w_down=w_down[l].astype(BF16), g_ffn_post=g_ffn_post[l])
        sink_heads = (sinks[l].astype(F32) * LOG2E).reshape(A_KV_HEADS, per).T.reshape(N_PAIRS, 2)

        def sink_rows(n):
            return jnp.broadcast_to(jnp.repeat(sink_heads, n, axis=1)[:, :, None], (N_PAIRS, 2 * n, LANES))

        bias = _build_bias(rel_bias[l])
        far = (B_REACH - REL_CLIP) // LANES * LANES
        bias_p = bias[:, :, far:].reshape(N_PAIRS, 2 * CHUNK, B_BAND_PAD - far)
        keys_s = cache_b_k.shape[2] + t_s
        bias_s = bias[:, :t_s, :keys_s].reshape(N_PAIRS, 2 * t_s, keys_s)

        keep_a, keep_b = min(A_WINDOW, t_p), min(B_REACH, t_p)
        (qa, ka, va, qb, kb, vb, sga, sgb, ak, av, bk, bv) = _in_projection(
            yp, g_mix_pre[l], w_in_bf, cos_p, sin_p, n_seq=n_p, seq_rows=t_p, tm=min(IN_PROJ_TILE, t_p),
            keep_a=keep_a, keep_b=keep_b, name="prompt_in_projection")
        oa, ob = _prompt_attention(qa, ka, va, qb, kb, vb, bias_p, sink_rows(CHUNK),
                                   n_seq=n_p, seq_rows=t_p, tq=min(ATTN_TILE, t_p))
        yp, conv_p = _post(yp, oa, ob, sga, sgb, w, n_seq=n_p, seq_rows=t_p, tm=min(ROW_TILE, t_p),
                           name="prompt_post")
        new_p.append((ak.reshape(n_p, keep_a, A_KV_HEADS, HEAD_DIM), av.reshape(n_p, keep_a, A_KV_HEADS, HEAD_DIM),
                      bk.reshape(n_p, keep_b, B_HEADS, HEAD_DIM), bv.reshape(n_p, keep_b, B_HEADS, HEAD_DIM),
                      conv_p))

        rows_s = n_s * t_s
        (qa, ka, va, qb, kb, vb, sga, sgb, ak, av, bk, bv) = _in_projection(
            ys, g_mix_pre[l], w_in_bf, cos_s, sin_s, n_seq=1, seq_rows=rows_s, tm=rows_s,
            keep_a=rows_s, keep_b=rows_s, name="sample_in_projection")
        flat_heads = lambda c: c.reshape(c.shape[0], c.shape[1], -1).astype(BF16)
        oa, ob = _sample_attention(qa, ka, va, flat_heads(cache_a_k[l]), flat_heads(cache_a_v[l]),
                                   qb, kb, vb, flat_heads(cache_b_k[l]), flat_heads(cache_b_v[l]),
                                   bias_s, sink_rows(t_s), n_seq=n_s, seq_rows=t_s)
        st = state_conv[l].astype(F32)
        zeros = jnp.zeros((n_s, t_s - 2, d_ff), F32)
        prev1 = jnp.concatenate([st[:, 1:2], zeros, zeros[:, :1]], axis=1).reshape(rows_s, d_ff)
        prev2 = jnp.concatenate([st, zeros], axis=1).reshape(rows_s, d_ff)
        ys, u_s = _post(ys, oa, ob, sga, sgb, w, n_seq=n_s, seq_rows=t_s, tm=rows_s,
                        conv_bounds=(prev1, prev2), name="sample_post")
        new_s.append((ak.reshape(n_s, t_s, A_KV_HEADS, HEAD_DIM), av.reshape(n_s, t_s, A_KV_HEADS, HEAD_DIM),
                      bk.reshape(n_s, t_s, B_HEADS, HEAD_DIM), bv.reshape(n_s, t_s, B_HEADS, HEAD_DIM),
                      u_s.reshape(n_s, t_s, d_ff)[:, t_s - (CONV_WIDTH - 1):]))

    stack = lambda lst, k: jnp.stack([s[k] for s in lst], axis=0)
    return (yp.reshape(n_p, t_p, d_model), ys.reshape(n_s, t_s, d_model),
            stack(new_p, 0), stack(new_p, 1), stack(new_p, 2), stack(new_p, 3), stack(new_p, 4),
            stack(new_s, 0), stack(new_s, 1), stack(new_s, 2), stack(new_s, 3), stack(new_s, 4))
```

```python
import functools

import jax
import jax.numpy as jnp
from jax import lax
from jax.experimental import pallas as pl
from jax.experimental.pallas import tpu as pltpu

CHUNK = 64
HEAD_DIM = 64
A_Q_HEADS = 8
A_KV_HEADS = 2
A_WINDOW = 128
B_HEADS = 8
B_REACH = 512
REL_CLIP = 128
ROPE_THETA = 10000.0
CONV_WIDTH = 3
RMS_EPS = 1e-6
NEG_INF = -1e30
PAST_LEN = 1024

A_Q_W = A_Q_HEADS * HEAD_DIM
A_KV_W = A_KV_HEADS * HEAD_DIM
B_W = B_HEADS * HEAD_DIM
LANES = 128
N_PAIRS = A_Q_W // LANES
A_BAND = A_WINDOW + CHUNK
B_BAND = B_REACH + CHUNK
B_BAND_PAD = -(-B_BAND // LANES) * LANES
TOEPLITZ_W = 1 << (B_BAND_PAD + CHUNK - 1).bit_length()
LOG2E = 1.4426950408889634
ROW_TILE = 512
IN_PROJ_TILE = 2 * ROW_TILE
PART_ROWS = 256
ATTN_TILE = 2 * ROW_TILE
STEP_CHUNKS = 4
FF_CHUNK = 2816
VMEM_LIMIT = 56 * 1024 * 1024

BF16 = jnp.bfloat16
F32 = jnp.float32


def _resident(shape):
    nd = len(shape)
    return pl.BlockSpec(shape, lambda *_: (0,) * nd, pipeline_mode=pl.Buffered(1))


def _rms(x, g):
    return (x * lax.rsqrt(jnp.mean(x * x, axis=-1, keepdims=True) + RMS_EPS)) * g


def _bias_kernel(t_ref, o_ref):
    row = lax.broadcasted_iota(jnp.int32, (CHUNK, TOEPLITZ_W), 0)
    col = lax.broadcasted_iota(jnp.int32, (CHUNK, B_BAND_PAD), 1)
    for head in range(t_ref.shape[0]):
        x = jnp.broadcast_to(t_ref[head], (CHUNK, TOEPLITZ_W))
        for b in range(CHUNK.bit_length() - 1):
            x = jnp.where((row >> b) & 1 == 1, pltpu.roll(x, 1 << b, axis=1), x)
        o_ref[head] = jnp.where(col < B_BAND, x[:, :B_BAND_PAD], NEG_INF)


def _build_bias(rel_table):
    h = rel_table.shape[0]
    tab = (rel_table.astype(F32) - rel_table[:, :1].astype(F32)) * LOG2E
    flat = B_REACH - REL_CLIP
    ramp = CHUNK + REL_CLIP
    row = jnp.concatenate([
        jnp.broadcast_to(tab[:, :1], (h, flat)),
        tab[:, :ramp],
        jnp.broadcast_to(tab[:, :1], (h, TOEPLITZ_W - flat - ramp)),
    ], axis=1).reshape(h, 1, TOEPLITZ_W)
    return pl.pallas_call(
        _bias_kernel,
        grid=(1,),
        in_specs=[pl.BlockSpec((h, 1, TOEPLITZ_W), lambda i: (0, 0, 0))],
        out_specs=pl.BlockSpec((h, CHUNK, B_BAND_PAD), lambda i: (0, 0, 0)),
        out_shape=jax.ShapeDtypeStruct((h, CHUNK, B_BAND_PAD), F32),
        name="rel_bias_tile",
    )(row)


def _rope(r, cos, sin_signed):
    lane = lax.broadcasted_iota(jnp.int32, (r.shape[0], LANES), 1)
    first_half = (lane & (HEAD_DIM // 2)) == 0
    out = []
    for j in range(r.shape[1] // LANES):
        blk = r[:, j * LANES:(j + 1) * LANES]
        up = pltpu.roll(blk, HEAD_DIM // 2, axis=1)
        down = pltpu.roll(blk, LANES - HEAD_DIM // 2, axis=1)
        out.append(blk * cos + jnp.where(first_half, down, up) * sin_signed)
    return out[0] if len(out) == 1 else jnp.concatenate(out, axis=1)


def _inproj_kernel(x_ref, g_ref, w_ref, cos_ref, sin_ref,
                   qa_ref, ka_ref, va_ref, qb_ref, kb_ref, vb_ref, sga_ref, sgb_ref,
                   ak_ref, av_ref, bk_ref, bv_ref, *, d_model, keep_a, keep_b, tm, n_split):
    i = pl.program_id(1)
    nt = pl.num_programs(1)
    scale = HEAD_DIM ** -0.5 * LOG2E
    part = tm // n_split

    def keep_tail(dst_ref, val, keep, r0):
        if keep <= tm:
            lo = max(r0, tm - keep)
            if lo < r0 + part:
                dst_ref[lo - (tm - keep):r0 + part - (tm - keep), :] = val[lo - r0:, :]
        else:
            first = nt - keep // tm

            @pl.when(i >= first)
            def _():
                dst_ref[pl.ds(pl.multiple_of((i - first) * tm + r0, part), part), :] = val

    for r0 in range(0, tm, part):
        rows = slice(r0, r0 + part)
        h = _rms(x_ref[rows, :], g_ref[...]).astype(BF16)
        r = jnp.dot(h, w_ref[...], preferred_element_type=F32)
        cos = cos_ref[rows, :]
        sin = sin_ref[rows, :]
        off = 0
        qa_ref[rows, :] = (_rope(r[:, off:off + A_Q_W], cos, sin) * scale).astype(BF16)
        off += A_Q_W
        ka = _rope(r[:, off:off + A_KV_W], cos, sin)
        ka_ref[rows, :] = ka.astype(BF16)
        keep_tail(ak_ref, ka, keep_a, r0)
        off += A_KV_W
        va = r[:, off:off + A_KV_W]
        va_ref[rows, :] = va.astype(BF16)
        keep_tail(av_ref, va, keep_a, r0)
        off += A_KV_W
        qb_ref[rows, :] = (r[:, off:off + B_W] * scale).astype(BF16)
        off += B_W
        kb = r[:, off:off + B_W]
        kb_ref[rows, :] = kb.astype(BF16)
        keep_tail(bk_ref, kb, keep_b, r0)
        off += B_W
        vb = r[:, off:off + B_W]
        vb_ref[rows, :] = vb.astype(BF16)
        keep_tail(bv_ref, vb, keep_b, r0)
        off += B_W
        sga_ref[rows, :] = jax.nn.sigmoid(r[:, off:off + d_model]).astype(BF16)
        off += d_model
        sgb_ref[rows, :] = jax.nn.sigmoid(r[:, off:off + d_model]).astype(BF16)


def _in_projection(x2d, g, w_in_bf, cos, sin, *, n_seq, seq_rows, tm, keep_a, keep_b, name):
    rows, d_model = x2d.shape
    nt = seq_rows // tm
    tok = lambda width: pl.BlockSpec((tm, width), lambda b, i: (b * nt + i, 0))
    tail = lambda keep, width: pl.BlockSpec((None, keep, width), lambda b, i: (b, 0, 0))
    tok_shape = lambda width: jax.ShapeDtypeStruct((rows, width), BF16)
    tail_shape = lambda keep, width: jax.ShapeDtypeStruct((n_seq, keep, width), F32)
    return pl.pallas_call(
        functools.partial(_inproj_kernel, d_model=d_model, keep_a=keep_a, keep_b=keep_b, tm=tm,
                          n_split=max(1, tm // PART_ROWS)),
        grid=(n_seq, nt),
        in_specs=[
            pl.BlockSpec((tm, d_model), lambda b, i: (b * nt + i, 0)),
            _resident((1, d_model)),
            _resident(w_in_bf.shape),
            pl.BlockSpec((tm, LANES), lambda b, i: (i, 0)),
            pl.BlockSpec((tm, LANES), lambda b, i: (i, 0)),
        ],
        out_specs=[tok(A_Q_W), tok(A_KV_W), tok(A_KV_W), tok(B_W), tok(B_W), tok(B_W),
                   tok(d_model), tok(d_model),
                   tail(keep_a, A_KV_W), tail(keep_a, A_KV_W), tail(keep_b, B_W), tail(keep_b, B_W)],
        out_shape=[tok_shape(A_Q_W), tok_shape(A_KV_W), tok_shape(A_KV_W), tok_shape(B_W), tok_shape(B_W),
                   tok_shape(B_W), tok_shape(d_model), tok_shape(d_model),
                   tail_shape(keep_a, A_KV_W), tail_shape(keep_a, A_KV_W),
                   tail_shape(keep_b, B_W), tail_shape(keep_b, B_W)],
        compiler_params=pltpu.CompilerParams(
            dimension_semantics=("arbitrary", "arbitrary"), vmem_limit_bytes=VMEM_LIMIT),
        name=name,
    )(x2d, g.reshape(1, d_model).astype(F32), w_in_bf, cos, sin)


def _stack_pair(q):
    lo = lax.broadcasted_iota(jnp.int32, q.shape, 1) < HEAD_DIM
    zero = jnp.zeros_like(q)
    return jnp.concatenate([jnp.where(lo, q, zero), jnp.where(lo, zero, q)], axis=0)


def _unstack_pair(o):
    n = o.shape[0] // 2
    lo = lax.broadcasted_iota(jnp.int32, (n, LANES), 1) < HEAD_DIM
    return jnp.where(lo, o[:n], o[n:])


def _scores(q2, k, *, bias=None, valid=None):
    s = lax.dot_general(q2, k, (((1,), (1,)), ((), ())), preferred_element_type=F32)
    if bias is not None:
        lead = s.shape[1] - bias.shape[1]
        s = s + bias if lead == 0 else jnp.concatenate([s[:, :lead], s[:, lead:] + bias], axis=1)
    if valid is not None:
        s = jnp.where(valid, s, NEG_INF)
    return s


def _row_max(s, sink=None):
    m = jnp.broadcast_to(jnp.max(s, axis=-1, keepdims=True), (s.shape[0], LANES))
    return m if sink is None else jnp.maximum(m, sink)


def _softmax_terms(s, m, *, sink=None):
    nk = s.shape[1]
    p = jnp.concatenate([jnp.exp2(s[:, k:min(k + LANES, nk)] - m[:, :min(LANES, nk - k)])
                         for k in range(0, nk, LANES)], axis=1)
    denom = jnp.broadcast_to(jnp.sum(p, axis=-1, keepdims=True), m.shape)
    if sink is not None:
        denom = denom + jnp.exp2(sink - m)
    return p.astype(BF16), 1.0 / denom


def _softmax_pv(s, m, v, *, sink=None):
    p, recip = _softmax_terms(s, m, sink=sink)
    return jnp.dot(p, v, preferred_element_type=F32) * recip


def _pair_attention(q, k, v, *, bias=None, sink=None):
    s = _scores(_stack_pair(q), k, bias=bias)
    return _unstack_pair(_softmax_pv(s, _row_max(s, sink), v, sink=sink))


def _prompt_attn_kernel(qa_ref, ka_ref, va_ref, qb_ref, kb_ref, vb_ref, bias_ref, sink_ref,
                        oa_ref, ob_ref, kap_all, vap_all, kbp_all, vbp_all, kv_sem,
                        sa_scr, ma_scr, sb_scr, mb_scr, *, seq_rows, tq):
    b = pl.program_id(0)
    i = pl.program_id(1)
    n_seq = pl.num_programs(0)
    slot = b % 2

    def kv_copies(seq, dst_slot):
        pairs = [(ka_ref, kap_all, A_WINDOW), (va_ref, vap_all, A_WINDOW),
                 (kb_ref, kbp_all, B_REACH), (vb_ref, vbp_all, B_REACH)]
        return [pltpu.make_async_copy(src.at[seq], dst.at[dst_slot, pl.ds(pad, seq_rows), :],
                                      kv_sem.at[dst_slot, n])
                for n, (src, dst, pad) in enumerate(pairs)]

    @pl.when(i == 0)
    def _():
        @pl.when(b == 0)
        def _():
            for s in range(2):
                kap_all[s, 0:A_WINDOW, :] = jnp.zeros((A_WINDOW, LANES), BF16)
                vap_all[s, 0:A_WINDOW, :] = jnp.zeros((A_WINDOW, LANES), BF16)
                kbp_all[s, 0:B_REACH, :] = jnp.zeros((B_REACH, B_W), BF16)
                vbp_all[s, 0:B_REACH, :] = jnp.zeros((B_REACH, B_W), BF16)
                kbp_all[s, B_REACH + seq_rows:, :] = jnp.zeros((B_BAND_PAD - B_BAND, B_W), BF16)
                vbp_all[s, B_REACH + seq_rows:, :] = jnp.zeros((B_BAND_PAD - B_BAND, B_W), BF16)
            for copy in kv_copies(0, 0):
                copy.start()

        for copy in kv_copies(b, slot):
            copy.wait()

        @pl.when(b + 1 < n_seq)
        def _():
            for copy in kv_copies(b + 1, 1 - slot):
                copy.start()

    kap, vap, kbp, vbp = kap_all.at[slot], vap_all.at[slot], kbp_all.at[slot], vbp_all.at[slot]

    col_a = lax.broadcasted_iota(jnp.int32, (2 * CHUNK, A_BAND), 1)
    col_b = lax.broadcasted_iota(jnp.int32, (2 * CHUNK, B_BAND_PAD), 1)

    n_chunks = tq // CHUNK

    def band_start(cc):
        start = (i * n_chunks + cc) * CHUNK
        return start, pl.multiple_of(cc * CHUNK, CHUNK), pl.multiple_of(start, CHUNK)

    def score_stage(cc, buf, masked):
        start, r0, w0 = band_start(cc)
        valid_a = (col_a >= A_WINDOW - start) if masked else None
        valid_b = (col_b >= B_REACH - start) if masked else None
        for j in range(N_PAIRS):
            sl = slice(j * LANES, (j + 1) * LANES)
            s = _scores(_stack_pair(qa_ref[pl.ds(r0, CHUNK), sl]), kap[pl.ds(w0, A_BAND), :], valid=valid_a)
            sa_scr[buf, j] = s
            ma_scr[buf, j] = _row_max(s, sink_ref[j])
        for j in range(N_PAIRS):
            sl = slice(j * LANES, (j + 1) * LANES)
            s = _scores(_stack_pair(qb_ref[pl.ds(r0, CHUNK), sl]), kbp[pl.ds(w0, B_BAND_PAD), sl],
                        bias=bias_ref[j], valid=valid_b)
            sb_scr[buf, j] = s
            mb_scr[buf, j] = _row_max(s)

    def output_stage(cc, buf):
        _, r0, w0 = band_start(cc)
        for j in range(N_PAIRS):
            sl = slice(j * LANES, (j + 1) * LANES)
            o = _softmax_pv(sa_scr[buf, j], ma_scr[buf, j], vap[pl.ds(w0, A_BAND), :], sink=sink_ref[j])
            oa_ref[pl.ds(r0, CHUNK), sl] = _unstack_pair(o.astype(BF16))
        for j in range(N_PAIRS):
            sl = slice(j * LANES, (j + 1) * LANES)
            o = _softmax_pv(sb_scr[buf, j], mb_scr[buf, j], vbp[pl.ds(w0, B_BAND_PAD), sl])
            ob_ref[pl.ds(r0, CHUNK), sl] = _unstack_pair(o.astype(BF16))

    def run(masked):
        k = STEP_CHUNKS
        n_steps = n_chunks // k
        for c in range(k):
            score_stage(c, c, masked)

        def step(t, carry):
            for parity in (0, 1):
                @pl.when(t % 2 == parity)
                def _():
                    for c in range(k):
                        output_stage(k * t + c, k * parity + c)
                    for c in range(k):
                        score_stage(k * (t + 1) + c, k * (1 - parity) + c, masked)
            return carry

        lax.fori_loop(0, n_steps - 1, step, 0)
        for c in range(k):
            output_stage(n_chunks - k + c, k * ((n_steps - 1) % 2) + c)

    run(True)


def _prompt_attention(qa, ka, va, qb, kb, vb, bias_pairs, sink_pairs, *, n_seq, seq_rows, tq):
    rows = n_seq * seq_rows
    nt = seq_rows // tq
    qspec = lambda width: pl.BlockSpec((tq, width), lambda b, i: (b * nt + i, 0))
    kvspec = pl.BlockSpec(memory_space=pl.ANY)
    seq3 = lambda a: a.reshape(n_seq, seq_rows, a.shape[-1])
    return pl.pallas_call(
        functools.partial(_prompt_attn_kernel, seq_rows=seq_rows, tq=tq),
        grid=(n_seq, nt),
        in_specs=[qspec(A_Q_W), kvspec, kvspec, qspec(B_W), kvspec, kvspec,
                  _resident(bias_pairs.shape), _resident(sink_pairs.shape)],
        out_specs=[qspec(A_Q_W), qspec(B_W)],
        out_shape=[jax.ShapeDtypeStruct((rows, A_Q_W), BF16), jax.ShapeDtypeStruct((rows, B_W), BF16)],
        scratch_shapes=[pltpu.VMEM((2, A_WINDOW + seq_rows, A_KV_W), BF16),
                        pltpu.VMEM((2, A_WINDOW + seq_rows, A_KV_W), BF16),
                        pltpu.VMEM((2, B_REACH + seq_rows + B_BAND_PAD - B_BAND, B_W), BF16),
                        pltpu.VMEM((2, B_REACH + seq_rows + B_BAND_PAD - B_BAND, B_W), BF16),
                        pltpu.SemaphoreType.DMA((2, 4)),
                        pltpu.VMEM((2 * STEP_CHUNKS, N_PAIRS, 2 * CHUNK, A_BAND), F32),
                        pltpu.VMEM((2 * STEP_CHUNKS, N_PAIRS, 2 * CHUNK, LANES), F32),
                        pltpu.VMEM((2 * STEP_CHUNKS, N_PAIRS, 2 * CHUNK, B_BAND_PAD), F32),
                        pltpu.VMEM((2 * STEP_CHUNKS, N_PAIRS, 2 * CHUNK, LANES), F32)],
        compiler_params=pltpu.CompilerParams(
            dimension_semantics=("arbitrary", "arbitrary"), vmem_limit_bytes=VMEM_LIMIT),
        name="prompt_attention",
    )(qa, seq3(ka), seq3(va), qb, seq3(kb), seq3(vb), bias_pairs, sink_pairs)


def _sample_attn_kernel(qa_ref, ka_ref, va_ref, cak_ref, cav_ref, qb_ref, kb_ref, vb_ref, cbk_ref, cbv_ref,
                        bias_ref, sink_ref, oa_ref, ob_ref):
    ka = jnp.concatenate([cak_ref[...], ka_ref[...]], axis=0)
    va = jnp.concatenate([cav_ref[...], va_ref[...]], axis=0)
    for j in range(N_PAIRS):
        sl = slice(j * LANES, (j + 1) * LANES)
        oa_ref[:, sl] = _pair_attention(qa_ref[:, sl], ka, va, sink=sink_ref[j]).astype(BF16)
    for j in range(N_PAIRS):
        sl = slice(j * LANES, (j + 1) * LANES)
        kb = jnp.concatenate([cbk_ref[:, sl], kb_ref[:, sl]], axis=0)
        vb = jnp.concatenate([cbv_ref[:, sl], vb_ref[:, sl]], axis=0)
        ob_ref[:, sl] = _pair_attention(qb_ref[:, sl], kb, vb, bias=bias_ref[j]).astype(BF16)


def _sample_attention(qa, ka, va, cak, cav, qb, kb, vb, cbk, cbv, bias_pairs, sink_pairs, *, n_seq, seq_rows):
    rows = n_seq * seq_rows
    new = lambda width: pl.BlockSpec((seq_rows, width), lambda b: (b, 0))
    past = lambda a: pl.BlockSpec((None,) + a.shape[1:], lambda b: (b, 0, 0))
    return pl.pallas_call(
        _sample_attn_kernel,
        grid=(n_seq,),
        in_specs=[new(A_Q_W), new(A_KV_W), new(A_KV_W), past(cak), past(cav),
                  new(B_W), new(B_W), new(B_W), past(cbk), past(cbv),
                  _resident(bias_pairs.shape), _resident(sink_pairs.shape)],
        out_specs=[new(A_Q_W), new(B_W)],
        out_shape=[jax.ShapeDtypeStruct((rows, A_Q_W), BF16), jax.ShapeDtypeStruct((rows, B_W), BF16)],
        compiler_params=pltpu.CompilerParams(dimension_semantics=("arbitrary",)),
        name="sample_attention",
    )(qa, ka, va, cak, cav, qb, kb, vb, cbk, cbv, bias_pairs, sink_pairs)


def _ff_chunks(d_ff):
    return [(lo, min(FF_CHUNK, d_ff - lo)) for lo in range(0, d_ff, FF_CHUNK)]


def _interleave_up(w_up, d_ff):
    return jnp.concatenate([w_up[:, base + lo:base + lo + width]
                            for lo, width in _ff_chunks(d_ff) for base in (0, d_ff)], axis=1)


def _post_kernel(*refs, tm, d_ff, seq_rows, carry_conv):
    if carry_conv:
        (x_ref, oa_ref, ob_ref, sga_ref, sgb_ref, woa_ref, wob_ref, wout_ref, gpost_ref, gpre_ref,
         wup_ref, cw_ref, cb_ref, wdown_ref, gffn_ref, y_ref, conv_ref, h2_ref) = refs
    else:
        (x_ref, oa_ref, ob_ref, sga_ref, sgb_ref, woa_ref, wob_ref, wout_ref, gpost_ref, gpre_ref,
         wup_ref, cw_ref, cb_ref, wdown_ref, gffn_ref, p1_ref, p2_ref, y_ref, u_ref, h2_ref) = refs
    i = pl.program_id(1)

    if carry_conv:
        @pl.when(i == 0)
        def _():
            conv_ref[...] = jnp.zeros_like(conv_ref)

    mixed = (sga_ref[...].astype(F32) * jnp.dot(oa_ref[...], woa_ref[...], preferred_element_type=F32)
             + sgb_ref[...].astype(F32) * jnp.dot(ob_ref[...], wob_ref[...], preferred_element_type=F32))
    z = jnp.dot(mixed.astype(BF16), wout_ref[...], preferred_element_type=F32)
    x1 = x_ref[...] + _rms(z, gpost_ref[...])
    y_ref[...] = x1
    h2_ref[...] = _rms(x1, gpre_ref[...]).astype(BF16)

    sqrt_half = 0.7071067811865476
    ff = None
    for lo, width in _ff_chunks(d_ff):
        cols = slice(lo, lo + width)
        uv = jnp.dot(h2_ref[...], wup_ref[:, 2 * lo:2 * (lo + width)], preferred_element_type=F32)
        u = uv[:, :width]
        val = uv[:, width:]
        row = lax.broadcasted_iota(jnp.int32, (tm, width), 0)
        back1 = pltpu.roll(u, 1, axis=0)
        back2 = pltpu.roll(u, 2, axis=0)
        if carry_conv:
            c0 = conv_ref[0:1, cols]
            c1 = conv_ref[1:2, cols]
            back1 = jnp.where(row == 0, c1, back1)
            back2 = jnp.where(row == 0, c0, jnp.where(row == 1, c1, back2))
            conv_ref[:, cols] = u[tm - (CONV_WIDTH - 1):, :]
        else:
            row_in_seq = row % seq_rows
            back1 = jnp.where(row_in_seq == 0, p1_ref[:, cols], back1)
            back2 = jnp.where(row_in_seq < 2, p2_ref[:, cols], back2)
            u_ref[:, cols] = u
        conv = cb_ref[:, cols] + back2 * cw_ref[0:1, cols]
        conv = conv + back1 * cw_ref[1:2, cols]
        conv = conv + u * cw_ref[2:3, cols]
        act = 0.5 * conv * (1.0 + lax.erf(conv * sqrt_half))
        part = jnp.dot((act * val).astype(BF16), wdown_ref[cols, :], preferred_element_type=F32)
        ff = part if ff is None else ff + part
    y_ref[...] = y_ref[...] + _rms(ff, gffn_ref[...])


def _post(x2d, oa, ob, sga, sgb, w, *, n_seq, seq_rows, tm, conv_bounds=None, name):
    rows, d_model = x2d.shape
    d_ff = w["w_down"].shape[0]
    carry_conv = conv_bounds is None
    nt = seq_rows // tm if carry_conv else 1
    grid = (n_seq, nt) if carry_conv else (1, 1)
    tok = lambda width: pl.BlockSpec((tm, width), lambda b, i: (b * nt + i, 0))
    vec = lambda a: a.reshape(1, -1).astype(F32)
    weights = [w["w_oa"], w["w_ob"], w["w_out"], vec(w["g_mix_post"]), vec(w["g_ffn_pre"]), w["w_up"],
               w["conv_w"].astype(F32), vec(w["conv_b"]), w["w_down"], vec(w["g_ffn_post"])]
    in_specs = [tok(d_model), tok(A_Q_W), tok(B_W), tok(d_model), tok(d_model)] + [_resident(a.shape) for a in weights]
    args = [x2d, oa, ob, sga, sgb] + weights
    if carry_conv:
        side_spec = pl.BlockSpec((None, CONV_WIDTH - 1, d_ff), lambda b, i: (b, 0, 0))
        side_shape = jax.ShapeDtypeStruct((n_seq, CONV_WIDTH - 1, d_ff), F32)
    else:
        in_specs += [tok(d_ff), tok(d_ff)]
        args += list(conv_bounds)
        side_spec = tok(d_ff)
        side_shape = jax.ShapeDtypeStruct((rows, d_ff), F32)
    return pl.pallas_call(
        functools.partial(_post_kernel, tm=tm, d_ff=d_ff, seq_rows=seq_rows, carry_conv=carry_conv),
        grid=grid,
        in_specs=in_specs,
        out_specs=[tok(d_model), side_spec],
        out_shape=[jax.ShapeDtypeStruct((rows, d_model), F32), side_shape],
        scratch_shapes=[pltpu.VMEM((tm, d_model), BF16)],
        compiler_params=pltpu.CompilerParams(
            dimension_semantics=("arbitrary", "arbitrary"), vmem_limit_bytes=VMEM_LIMIT),
        name=name,
    )(*args)


def _rope_tables(pos):
    half = HEAD_DIM // 2
    inv = 1.0 / (ROPE_THETA ** (jnp.arange(half, dtype=F32) * (2.0 / HEAD_DIM)))
    ang = pos.astype(F32)[:, None] * inv[None, :]
    cos, sin = jnp.cos(ang), jnp.sin(ang)
    reps = LANES // HEAD_DIM
    return (jnp.tile(jnp.concatenate([cos, cos], axis=1), (1, reps)),
            jnp.tile(jnp.concatenate([-sin, sin], axis=1), (1, reps)))


def _pair_heads_cols(w_q):
    d = w_q.shape[0]
    per = A_Q_HEADS // A_KV_HEADS
    return w_q.reshape(d, A_KV_HEADS, per, HEAD_DIM).transpose(0, 2, 1, 3).reshape(d, A_Q_W)


def kernel(x_prompt, x_sample, cache_a_k, cache_a_v, cache_b_k, cache_b_v, state_conv, g_mix_pre, w_in, sinks,
           rel_bias, w_oa, w_ob, w_out, g_mix_post, g_ffn_pre, w_up, conv_w, conv_b, w_down, g_ffn_post):
    depth = w_in.shape[0]
    n_p, t_p, d_model = x_prompt.shape
    n_s, t_s, _ = x_sample.shape
    d_ff = w_down.shape[1]
    per = A_Q_HEADS // A_KV_HEADS

    cos_p, sin_p = _rope_tables(jnp.arange(t_p))
    cos_s, sin_s = _rope_tables(jnp.tile(PAST_LEN + jnp.arange(t_s), n_s))

    yp = x_prompt.reshape(n_p * t_p, d_model)
    ys = x_sample.reshape(n_s * t_s, d_model)
    new_p, new_s = [], []
    for l in range(depth):
        w_in_bf = jnp.concatenate([_pair_heads_cols(w_in[l][:, :A_Q_W]), w_in[l][:, A_Q_W:]], axis=1).astype(BF16)
        w = dict(
            w_oa=w_oa[l].reshape(A_KV_HEADS, per, HEAD_DIM, d_model).transpose(1, 0, 2, 3)
                        .reshape(A_Q_W, d_model).astype(BF16),
            w_ob=w_ob[l].astype(BF16), w_out=w_out[l].astype(BF16), g_mix_post=g_mix_post[l],
            g_ffn_pre=g_ffn_pre[l], w_up=_interleave_up(w_up[l], d_ff).astype(BF16), conv_w=conv_w[l],
            conv_b=conv_b[l],
            w_down=w_down[l].astype(BF16), g_ffn_post=g_ffn_post[l])
        sink_heads = (sinks[l].astype(F32) * LOG2E).reshape(A_KV_HEADS, per).T.reshape(N_PAIRS, 2)

        def sink_rows(n):
            return jnp.broadcast_to(jnp.repeat(sink_heads, n, axis=1)[:, :, None], (N_PAIRS, 2 * n, LANES))

        bias = _build_bias(rel_bias[l])
        far = (B_REACH - REL_CLIP) // LANES * LANES
        bias_p = bias[:, :, far:].reshape(N_PAIRS, 2 * CHUNK, B_BAND_PAD - far)
        keys_s = cache_b_k.shape[2] + t_s
        bias_s = bias[:, :t_s, :keys_s].reshape(N_PAIRS, 2 * t_s, keys_s)

        keep_a, keep_b = min(A_WINDOW, t_p), min(B_REACH, t_p)
        (qa, ka, va, qb, kb, vb, sga, sgb, ak, av, bk, bv) = _in_projection(
            yp, g_mix_pre[l], w_in_bf, cos_p, sin_p, n_seq=n_p, seq_rows=t_p, tm=min(IN_PROJ_TILE, t_p),
            keep_a=keep_a, keep_b=keep_b, name="prompt_in_projection")
        oa, ob = _prompt_attention(qa, ka, va, qb, kb, vb, bias_p, sink_rows(CHUNK),
                                   n_seq=n_p, seq_rows=t_p, tq=min(ATTN_TILE, t_p))
        yp, conv_p = _post(yp, oa, ob, sga, sgb, w, n_seq=n_p, seq_rows=t_p, tm=min(ROW_TILE, t_p),
                           name="prompt_post")
        new_p.append((ak.reshape(n_p, keep_a, A_KV_HEADS, HEAD_DIM), av.reshape(n_p, keep_a, A_KV_HEADS, HEAD_DIM),
                      bk.reshape(n_p, keep_b, B_HEADS, HEAD_DIM), bv.reshape(n_p, keep_b, B_HEADS, HEAD_DIM),
                      conv_p))

        rows_s = n_s * t_s
        (qa, ka, va, qb, kb, vb, sga, sgb, ak, av, bk, bv) = _in_projection(
            ys, g_mix_pre[l], w_in_bf, cos_s, sin_s, n_seq=1, seq_rows=rows_s, tm=rows_s,
            keep_a=rows_s, keep_b=rows_s, name="sample_in_projection")
        flat_heads = lambda c: c.reshape(c.shape[0], c.shape[1], -1).astype(BF16)
        oa, ob = _sample_attention(qa, ka, va, flat_heads(cache_a_k[l]), flat_heads(cache_a_v[l]),
                                   qb, kb, vb, flat_heads(cache_b_k[l]), flat_heads(cache_b_v[l]),
                                   bias_s, sink_rows(t_s), n_seq=n_s, seq_rows=t_s)
        st = state_conv[l].astype(F32)
        zeros = jnp.zeros((n_s, t_s - 2, d_ff), F32)
        prev1 = jnp.concatenate([st[:, 1:2], zeros, zeros[:, :1]], axis=1).reshape(rows_s, d_ff)
        prev2 = jnp.concatenate([st, zeros], axis=1).reshape(rows_s, d_ff)
        ys, u_s = _post(ys, oa, ob, sga, sgb, w, n_seq=n_s, seq_rows=t_s, tm=rows_s,
                        conv_bounds=(prev1, prev2), name="sample_post")
        new_s.append((ak.reshape(n_s, t_s, A_KV_HEADS, HEAD_DIM), av.reshape(n_s, t_s, A_KV_HEADS, HEAD_DIM),
                      bk.reshape(n_s, t_s, B_HEADS, HEAD_DIM), bv.reshape(n_s, t_s, B_HEADS, HEAD_DIM),
                      u_s.reshape(n_s, t_s, d_ff)[:, t_s - (CONV_WIDTH - 1):]))

    stack = lambda lst, k: jnp.stack([s[k] for s in lst], axis=0)
    return (yp.reshape(n_p, t_p, d_model), ys.reshape(n_s, t_s, d_model),
            stack(new_p, 0), stack(new_p, 1), stack(new_p, 2), stack(new_p, 3), stack(new_p, 4),
            stack(new_s, 0), stack(new_s, 1), stack(new_s, 2), stack(new_s, 3), stack(new_s, 4))
```

```python
import functools

import jax
import jax.numpy as jnp
from jax import lax
from jax.experimental import pallas as pl
from jax.experimental.pallas import tpu as pltpu

CHUNK = 64
HEAD_DIM = 64
A_Q_HEADS = 8
A_KV_HEADS = 2
A_WINDOW = 128
B_HEADS = 8
B_REACH = 512
REL_CLIP = 128
ROPE_THETA = 10000.0
CONV_WIDTH = 3
RMS_EPS = 1e-6
NEG_INF = -1e30
PAST_LEN = 1024

A_Q_W = A_Q_HEADS * HEAD_DIM
A_KV_W = A_KV_HEADS * HEAD_DIM
B_W = B_HEADS * HEAD_DIM
LANES = 128
N_PAIRS = A_Q_W // LANES
A_BAND = A_WINDOW + CHUNK
B_BAND = B_REACH + CHUNK
B_BAND_PAD = -(-B_BAND // LANES) * LANES
TOEPLITZ_W = 1 << (B_BAND_PAD + CHUNK - 1).bit_length()
LOG2E = 1.4426950408889634
ROW_TILE = 512
IN_PROJ_TILE = 2 * ROW_TILE
PART_ROWS = 256
ATTN_TILE = 2 * ROW_TILE
STEP_CHUNKS = 4
FF_CHUNK = 2816
VMEM_LIMIT = 56 * 1024 * 1024

BF16 = jnp.bfloat16
F32 = jnp.float32


def _resident(shape):
    nd = len(shape)
    return pl.BlockSpec(shape, lambda *_: (0,) * nd, pipeline_mode=pl.Buffered(1))


def _rms(x, g):
    return (x * lax.rsqrt(jnp.mean(x * x, axis=-1, keepdims=True) + RMS_EPS)) * g


def _bias_kernel(t_ref, o_ref):
    row = lax.broadcasted_iota(jnp.int32, (CHUNK, TOEPLITZ_W), 0)
    col = lax.broadcasted_iota(jnp.int32, (CHUNK, B_BAND_PAD), 1)
    for head in range(t_ref.shape[0]):
        x = jnp.broadcast_to(t_ref[head], (CHUNK, TOEPLITZ_W))
        for b in range(CHUNK.bit_length() - 1):
            x = jnp.where((row >> b) & 1 == 1, pltpu.roll(x, 1 << b, axis=1), x)
        o_ref[head] = jnp.where(col < B_BAND, x[:, :B_BAND_PAD], NEG_INF)


def _build_bias(rel_table):
    h = rel_table.shape[0]
    tab = (rel_table.astype(F32) - rel_table[:, :1].astype(F32)) * LOG2E
    flat = B_REACH - REL_CLIP
    ramp = CHUNK + REL_CLIP
    row = jnp.concatenate([
        jnp.broadcast_to(tab[:, :1], (h, flat)),
        tab[:, :ramp],
        jnp.broadcast_to(tab[:, :1], (h, TOEPLITZ_W - flat - ramp)),
    ], axis=1).reshape(h, 1, TOEPLITZ_W)
    return pl.pallas_call(
        _bias_kernel,
        grid=(1,),
        in_specs=[pl.BlockSpec((h, 1, TOEPLITZ_W), lambda i: (0, 0, 0))],
        out_specs=pl.BlockSpec((h, CHUNK, B_BAND_PAD), lambda i: (0, 0, 0)),
        out_shape=jax.ShapeDtypeStruct((h, CHUNK, B_BAND_PAD), F32),
        name="rel_bias_tile",
    )(row)


def _rope(r, cos, sin_signed):
    lane = lax.broadcasted_iota(jnp.int32, (r.shape[0], LANES), 1)
    first_half = (lane & (HEAD_DIM // 2)) == 0
    out = []
    for j in range(r.shape[1] // LANES):
        blk = r[:, j * LANES:(j + 1) * LANES]
        up = pltpu.roll(blk, HEAD_DIM // 2, axis=1)
        down = pltpu.roll(blk, LANES - HEAD_DIM // 2, axis=1)
        out.append(blk * cos + jnp.where(first_half, down, up) * sin_signed)
    return out[0] if len(out) == 1 else jnp.concatenate(out, axis=1)


def _inproj_kernel(x_ref, g_ref, w_ref, cos_ref, sin_ref,
                   qa_ref, ka_ref, va_ref, qb_ref, kb_ref, vb_ref, sga_ref, sgb_ref,
                   ak_ref, av_ref, bk_ref, bv_ref, *, d_model, keep_a, keep_b, tm, n_split):
    i = pl.program_id(1)
    nt = pl.num_programs(1)
    scale = HEAD_DIM ** -0.5 * LOG2E
    part = tm // n_split

    def keep_tail(dst_ref, val, keep, r0):
        if keep <= tm:
            lo = max(r0, tm - keep)
            if lo < r0 + part:
                dst_ref[lo - (tm - keep):r0 + part - (tm - keep), :] = val[lo - r0:, :]
        else:
            first = nt - keep // tm

            @pl.when(i >= first)
            def _():
                dst_ref[pl.ds(pl.multiple_of((i - first) * tm + r0, part), part), :] = val

    for r0 in range(0, tm, part):
        rows = slice(r0, r0 + part)
        h = _rms(x_ref[rows, :], g_ref[...]).astype(BF16)
        r = jnp.dot(h, w_ref[...], preferred_element_type=F32)
        cos = cos_ref[rows, :]
        sin = sin_ref[rows, :]
        off = 0
        qa_ref[rows, :] = (_rope(r[:, off:off + A_Q_W], cos, sin) * scale).astype(BF16)
        off += A_Q_W
        ka = _rope(r[:, off:off + A_KV_W], cos, sin)
        ka_ref[rows, :] = ka.astype(BF16)
        keep_tail(ak_ref, ka, keep_a, r0)
        off += A_KV_W
        va = r[:, off:off + A_KV_W]
        va_ref[rows, :] = va.astype(BF16)
        keep_tail(av_ref, va, keep_a, r0)
        off += A_KV_W
        qb_ref[rows, :] = (r[:, off:off + B_W] * scale).astype(BF16)
        off += B_W
        kb = r[:, off:off + B_W]
        kb_ref[rows, :] = kb.astype(BF16)
        keep_tail(bk_ref, kb, keep_b, r0)
        off += B_W
        vb = r[:, off:off + B_W]
        vb_ref[rows, :] = vb.astype(BF16)
        keep_tail(bv_ref, vb, keep_b, r0)
        off += B_W
        sga_ref[rows, :] = jax.nn.sigmoid(r[:, off:off + d_model]).astype(BF16)
        off += d_model
        sgb_ref[rows, :] = jax.nn.sigmoid(r[:, off:off + d_model]).astype(BF16)


def _in_projection(x2d, g, w_in_bf, cos, sin, *, n_seq, seq_rows, tm, keep_a, keep_b, name):
    rows, d_model = x2d.shape
    nt = seq_rows // tm
    tok = lambda width: pl.BlockSpec((tm, width), lambda b, i: (b * nt + i, 0))
    tail = lambda keep, width: pl.BlockSpec((None, keep, width), lambda b, i: (b, 0, 0))
    tok_shape = lambda width: jax.ShapeDtypeStruct((rows, width), BF16)
    tail_shape = lambda keep, width: jax.ShapeDtypeStruct((n_seq, keep, width), F32)
    return pl.pallas_call(
        functools.partial(_inproj_kernel, d_model=d_model, keep_a=keep_a, keep_b=keep_b, tm=tm,
                          n_split=max(1, tm // PART_ROWS)),
        grid=(n_seq, nt),
        in_specs=[
            pl.BlockSpec((tm, d_model), lambda b, i: (b * nt + i, 0)),
            _resident((1, d_model)),
            _resident(w_in_bf.shape),
            pl.BlockSpec((tm, LANES), lambda b, i: (i, 0)),
            pl.BlockSpec((tm, LANES), lambda b, i: (i, 0)),
        ],
        out_specs=[tok(A_Q_W), tok(A_KV_W), tok(A_KV_W), tok(B_W), tok(B_W), tok(B_W),
                   tok(d_model), tok(d_model),
                   tail(keep_a, A_KV_W), tail(keep_a, A_KV_W), tail(keep_b, B_W), tail(keep_b, B_W)],
        out_shape=[tok_shape(A_Q_W), tok_shape(A_KV_W), tok_shape(A_KV_W), tok_shape(B_W), tok_shape(B_W),
                   tok_shape(B_W), tok_shape(d_model), tok_shape(d_model),
                   tail_shape(keep_a, A_KV_W), tail_shape(keep_a, A_KV_W),
                   tail_shape(keep_b, B_W), tail_shape(keep_b, B_W)],
        compiler_params=pltpu.CompilerParams(
            dimension_semantics=("arbitrary", "arbitrary"), vmem_limit_bytes=VMEM_LIMIT),
        name=name,
    )(x2d, g.reshape(1, d_model).astype(F32), w_in_bf, cos, sin)


def _stack_pair(q):
    lo = lax.broadcasted_iota(jnp.int32, q.shape, 1) < HEAD_DIM
    zero = jnp.zeros_like(q)
    return jnp.concatenate([jnp.where(lo, q, zero), jnp.where(lo, zero, q)], axis=0)


def _unstack_pair(o):
    n = o.shape[0] // 2
    lo = lax.broadcasted_iota(jnp.int32, (n, LANES), 1) < HEAD_DIM
    return jnp.where(lo, o[:n], o[n:])


def _scores(q2, k, *, bias=None, valid=None):
    s = lax.dot_general(q2, k, (((1,), (1,)), ((), ())), preferred_element_type=F32)
    if bias is not None:
        lead = s.shape[1] - bias.shape[1]
        s = s + bias if lead == 0 else jnp.concatenate([s[:, :lead], s[:, lead:] + bias], axis=1)
    if valid is not None:
        s = jnp.where(valid, s, NEG_INF)
    return s


def _row_max(s, sink=None):
    m = jnp.broadcast_to(jnp.max(s, axis=-1, keepdims=True), (s.shape[0], LANES))
    return m if sink is None else jnp.maximum(m, sink)


def _softmax_terms(s, m, *, sink=None):
    nk = s.shape[1]
    p = jnp.concatenate([jnp.exp2(s[:, k:min(k + LANES, nk)] - m[:, :min(LANES, nk - k)])
                         for k in range(0, nk, LANES)], axis=1)
    denom = jnp.broadcast_to(jnp.sum(p, axis=-1, keepdims=True), m.shape)
    if sink is not None:
        denom = denom + jnp.exp2(sink - m)
    return p.astype(BF16), 1.0 / denom


def _softmax_pv(s, m, v, *, sink=None):
    p, recip = _softmax_terms(s, m, sink=sink)
    return jnp.dot(p, v, preferred_element_type=F32) * recip


def _pair_attention(q, k, v, *, bias=None, sink=None):
    s = _scores(_stack_pair(q), k, bias=bias)
    return _unstack_pair(_softmax_pv(s, _row_max(s, sink), v, sink=sink))


def _prompt_attn_kernel(qa_ref, ka_ref, va_ref, qb_ref, kb_ref, vb_ref, bias_ref, sink_ref,
                        oa_ref, ob_ref, kap_all, vap_all, kbp_all, vbp_all, kv_sem,
                        sa_scr, ma_scr, sb_scr, mb_scr, *, seq_rows, tq):
    b = pl.program_id(0)
    i = pl.program_id(1)
    n_seq = pl.num_programs(0)
    slot = b % 2

    def kv_copies(seq, dst_slot):
        pairs = [(ka_ref, kap_all, A_WINDOW), (va_ref, vap_all, A_WINDOW),
                 (kb_ref, kbp_all, B_REACH), (vb_ref, vbp_all, B_REACH)]
        return [pltpu.make_async_copy(src.at[seq], dst.at[dst_slot, pl.ds(pad, seq_rows), :],
                                      kv_sem.at[dst_slot, n])
                for n, (src, dst, pad) in enumerate(pairs)]

    @pl.when(i == 0)
    def _():
        @pl.when(b == 0)
        def _():
            for s in range(2):
                kap_all[s, 0:A_WINDOW, :] = jnp.zeros((A_WINDOW, LANES), BF16)
                vap_all[s, 0:A_WINDOW, :] = jnp.zeros((A_WINDOW, LANES), BF16)
                kbp_all[s, 0:B_REACH, :] = jnp.zeros((B_REACH, B_W), BF16)
                vbp_all[s, 0:B_REACH, :] = jnp.zeros((B_REACH, B_W), BF16)
                kbp_all[s, B_REACH + seq_rows:, :] = jnp.zeros((B_BAND_PAD - B_BAND, B_W), BF16)
                vbp_all[s, B_REACH + seq_rows:, :] = jnp.zeros((B_BAND_PAD - B_BAND, B_W), BF16)
            for copy in kv_copies(0, 0):
                copy.start()

        for copy in kv_copies(b, slot):
            copy.wait()

        @pl.when(b + 1 < n_seq)
        def _():
            for n, copy in enumerate(kv_copies(b + 1, 1 - slot)):
                copy.start(priority=n % 2)

    kap, vap, kbp, vbp = kap_all.at[slot], vap_all.at[slot], kbp_all.at[slot], vbp_all.at[slot]

    col_a = lax.broadcasted_iota(jnp.int32, (2 * CHUNK, A_BAND), 1)
    col_b = lax.broadcasted_iota(jnp.int32, (2 * CHUNK, B_BAND_PAD), 1)

    n_chunks = tq // CHUNK

    def band_start(cc):
        start = (i * n_chunks + cc) * CHUNK
        return start, pl.multiple_of(cc * CHUNK, CHUNK), pl.multiple_of(start, CHUNK)

    def score_stage(cc, buf):
        start, r0, w0 = band_start(cc)
        valid_a = col_a >= A_WINDOW - start
        valid_b = col_b >= B_REACH - start
        for j in range(N_PAIRS):
            sl = slice(j * LANES, (j + 1) * LANES)
            s = _scores(_stack_pair(qa_ref[pl.ds(r0, CHUNK), sl]), kap[pl.ds(w0, A_BAND), :], valid=valid_a)
            sa_scr[buf, j] = s
            ma_scr[buf, j] = _row_max(s, sink_ref[j])
        for j in range(N_PAIRS):
            sl = slice(j * LANES, (j + 1) * LANES)
            s = _scores(_stack_pair(qb_ref[pl.ds(r0, CHUNK), sl]), kbp[pl.ds(w0, B_BAND_PAD), sl],
                        bias=bias_ref[j], valid=valid_b)
            sb_scr[buf, j] = s
            mb_scr[buf, j] = _row_max(s)

    def output_stage(cc, buf):
        _, r0, w0 = band_start(cc)
        for j in range(N_PAIRS):
            sl = slice(j * LANES, (j + 1) * LANES)
            o = _softmax_pv(sa_scr[buf, j], ma_scr[buf, j], vap[pl.ds(w0, A_BAND), :], sink=sink_ref[j])
            oa_ref[pl.ds(r0, CHUNK), sl] = _unstack_pair(o.astype(BF16))
        for j in range(N_PAIRS):
            sl = slice(j * LANES, (j + 1) * LANES)
            o = _softmax_pv(sb_scr[buf, j], mb_scr[buf, j], vbp[pl.ds(w0, B_BAND_PAD), sl])
            ob_ref[pl.ds(r0, CHUNK), sl] = _unstack_pair(o.astype(BF16))

    k = STEP_CHUNKS
    n_steps = n_chunks // k
    for c in range(k):
        score_stage(c, c)

    def step(t, carry):
        for parity in (0, 1):
            @pl.when(t % 2 == parity)
            def _():
                for c in range(k):
                    output_stage(k * t + c, k * parity + c)
                for c in range(k):
                    score_stage(k * (t + 1) + c, k * (1 - parity) + c)
        return carry

    lax.fori_loop(0, n_steps - 1, step, 0)
    for c in range(k):
        output_stage(n_chunks - k + c, k * ((n_steps - 1) % 2) + c)


def _prompt_attention(qa, ka, va, qb, kb, vb, bias_pairs, sink_pairs, *, n_seq, seq_rows, tq):
    rows = n_seq * seq_rows
    nt = seq_rows // tq
    qspec = lambda width: pl.BlockSpec((tq, width), lambda b, i: (b * nt + i, 0))
    kvspec = pl.BlockSpec(memory_space=pl.ANY)
    seq3 = lambda a: a.reshape(n_seq, seq_rows, a.shape[-1])
    return pl.pallas_call(
        functools.partial(_prompt_attn_kernel, seq_rows=seq_rows, tq=tq),
        grid=(n_seq, nt),
        in_specs=[qspec(A_Q_W), kvspec, kvspec, qspec(B_W), kvspec, kvspec,
                  _resident(bias_pairs.shape), _resident(sink_pairs.shape)],
        out_specs=[qspec(A_Q_W), qspec(B_W)],
        out_shape=[jax.ShapeDtypeStruct((rows, A_Q_W), BF16), jax.ShapeDtypeStruct((rows, B_W), BF16)],
        scratch_shapes=[pltpu.VMEM((2, A_WINDOW + seq_rows, A_KV_W), BF16),
                        pltpu.VMEM((2, A_WINDOW + seq_rows, A_KV_W), BF16),
                        pltpu.VMEM((2, B_REACH + seq_rows + B_BAND_PAD - B_BAND, B_W), BF16),
                        pltpu.VMEM((2, B_REACH + seq_rows + B_BAND_PAD - B_BAND, B_W), BF16),
                        pltpu.SemaphoreType.DMA((2, 4)),
                        pltpu.VMEM((2 * STEP_CHUNKS, N_PAIRS, 2 * CHUNK, A_BAND), F32),
                        pltpu.VMEM((2 * STEP_CHUNKS, N_PAIRS, 2 * CHUNK, LANES), F32),
                        pltpu.VMEM((2 * STEP_CHUNKS, N_PAIRS, 2 * CHUNK, B_BAND_PAD), F32),
                        pltpu.VMEM((2 * STEP_CHUNKS, N_PAIRS, 2 * CHUNK, LANES), F32)],
        compiler_params=pltpu.CompilerParams(
            dimension_semantics=("arbitrary", "arbitrary"), vmem_limit_bytes=VMEM_LIMIT),
        name="prompt_attention",
    )(qa, seq3(ka), seq3(va), qb, seq3(kb), seq3(vb), bias_pairs, sink_pairs)


def _sample_attn_kernel(qa_ref, ka_ref, va_ref, cak_ref, cav_ref, qb_ref, kb_ref, vb_ref, cbk_ref, cbv_ref,
                        bias_ref, sink_ref, oa_ref, ob_ref):
    ka = jnp.concatenate([cak_ref[...], ka_ref[...]], axis=0)
    va = jnp.concatenate([cav_ref[...], va_ref[...]], axis=0)
    for j in range(N_PAIRS):
        sl = slice(j * LANES, (j + 1) * LANES)
        oa_ref[:, sl] = _pair_attention(qa_ref[:, sl], ka, va, sink=sink_ref[j]).astype(BF16)
    for j in range(N_PAIRS):
        sl = slice(j * LANES, (j + 1) * LANES)
        kb = jnp.concatenate([cbk_ref[:, sl], kb_ref[:, sl]], axis=0)
        vb = jnp.concatenate([cbv_ref[:, sl], vb_ref[:, sl]], axis=0)
        ob_ref[:, sl] = _pair_attention(qb_ref[:, sl], kb, vb, bias=bias_ref[j]).astype(BF16)


def _sample_attention(qa, ka, va, cak, cav, qb, kb, vb, cbk, cbv, bias_pairs, sink_pairs, *, n_seq, seq_rows):
    rows = n_seq * seq_rows
    new = lambda width: pl.BlockSpec((seq_rows, width), lambda b: (b, 0))
    past = lambda a: pl.BlockSpec((None,) + a.shape[1:], lambda b: (b, 0, 0))
    return pl.pallas_call(
        _sample_attn_kernel,
        grid=(n_seq,),
        in_specs=[new(A_Q_W), new(A_KV_W), new(A_KV_W), past(cak), past(cav),
                  new(B_W), new(B_W), new(B_W), past(cbk), past(cbv),
                  _resident(bias_pairs.shape), _resident(sink_pairs.shape)],
        out_specs=[new(A_Q_W), new(B_W)],
        out_shape=[jax.ShapeDtypeStruct((rows, A_Q_W), BF16), jax.ShapeDtypeStruct((rows, B_W), BF16)],
        compiler_params=pltpu.CompilerParams(dimension_semantics=("arbitrary",)),
        name="sample_attention",
    )(qa, ka, va, cak, cav, qb, kb, vb, cbk, cbv, bias_pairs, sink_pairs)


def _ff_chunks(d_ff):
    return [(lo, min(FF_CHUNK, d_ff - lo)) for lo in range(0, d_ff, FF_CHUNK)]


def _interleave_up(w_up, d_ff):
    return jnp.concatenate([w_up[:, base + lo:base + lo + width]
                            for lo, width in _ff_chunks(d_ff) for base in (0, d_ff)], axis=1)


def _post_kernel(*refs, tm, d_ff, seq_rows, carry_conv):
    if carry_conv:
        (x_ref, oa_ref, ob_ref, sga_ref, sgb_ref, woa_ref, wob_ref, wout_ref, gpost_ref, gpre_ref,
         wup_ref, cw_ref, cb_ref, wdown_ref, gffn_ref, y_ref, conv_ref, h2_ref) = refs
    else:
        (x_ref, oa_ref, ob_ref, sga_ref, sgb_ref, woa_ref, wob_ref, wout_ref, gpost_ref, gpre_ref,
         wup_ref, cw_ref, cb_ref, wdown_ref, gffn_ref, p1_ref, p2_ref, y_ref, u_ref, h2_ref) = refs
    i = pl.program_id(1)

    if carry_conv:
        @pl.when(i == 0)
        def _():
            conv_ref[...] = jnp.zeros_like(conv_ref)

    mixed = (sga_ref[...].astype(F32) * jnp.dot(oa_ref[...], woa_ref[...], preferred_element_type=F32)
             + sgb_ref[...].astype(F32) * jnp.dot(ob_ref[...], wob_ref[...], preferred_element_type=F32))
    z = jnp.dot(mixed.astype(BF16), wout_ref[...], preferred_element_type=F32)
    x1 = x_ref[...] + _rms(z, gpost_ref[...])
    y_ref[...] = x1
    h2_ref[...] = _rms(x1, gpre_ref[...]).astype(BF16)

    sqrt_half = 0.7071067811865476
    ff = None
    for lo, width in _ff_chunks(d_ff):
        cols = slice(lo, lo + width)
        uv = jnp.dot(h2_ref[...], wup_ref[:, 2 * lo:2 * (lo + width)], preferred_element_type=F32)
        u = uv[:, :width]
        val = uv[:, width:]
        row = lax.broadcasted_iota(jnp.int32, (tm, width), 0)
        back1 = pltpu.roll(u, 1, axis=0)
        back2 = pltpu.roll(u, 2, axis=0)
        if carry_conv:
            c0 = conv_ref[0:1, cols]
            c1 = conv_ref[1:2, cols]
            back1 = jnp.where(row == 0, c1, back1)
            back2 = jnp.where(row == 0, c0, jnp.where(row == 1, c1, back2))
            conv_ref[:, cols] = u[tm - (CONV_WIDTH - 1):, :]
        else:
            row_in_seq = row % seq_rows
            back1 = jnp.where(row_in_seq == 0, p1_ref[:, cols], back1)
            back2 = jnp.where(row_in_seq < 2, p2_ref[:, cols], back2)
            u_ref[:, cols] = u
        conv = cb_ref[:, cols] + back2 * cw_ref[0:1, cols]
        conv = conv + back1 * cw_ref[1:2, cols]
        conv = conv + u * cw_ref[2:3, cols]
        act = 0.5 * conv * (1.0 + lax.erf(conv * sqrt_half))
        part = jnp.dot((act * val).astype(BF16), wdown_ref[cols, :], preferred_element_type=F32)
        ff = part if ff is None else ff + part
    y_ref[...] = y_ref[...] + _rms(ff, gffn_ref[...])


def _post(x2d, oa, ob, sga, sgb, w, *, n_seq, seq_rows, tm, conv_bounds=None, name):
    rows, d_model = x2d.shape
    d_ff = w["w_down"].shape[0]
    carry_conv = conv_bounds is None
    nt = seq_rows // tm if carry_conv else 1
    grid = (n_seq, nt) if carry_conv else (1, 1)
    tok = lambda width: pl.BlockSpec((tm, width), lambda b, i: (b * nt + i, 0))
    vec = lambda a: a.reshape(1, -1).astype(F32)
    weights = [w["w_oa"], w["w_ob"], w["w_out"], vec(w["g_mix_post"]), vec(w["g_ffn_pre"]), w["w_up"],
               w["conv_w"].astype(F32), vec(w["conv_b"]), w["w_down"], vec(w["g_ffn_post"])]
    in_specs = [tok(d_model), tok(A_Q_W), tok(B_W), tok(d_model), tok(d_model)] + [_resident(a.shape) for a in weights]
    args = [x2d, oa, ob, sga, sgb] + weights
    if carry_conv:
        side_spec = pl.BlockSpec((None, CONV_WIDTH - 1, d_ff), lambda b, i: (b, 0, 0))
        side_shape = jax.ShapeDtypeStruct((n_seq, CONV_WIDTH - 1, d_ff), F32)
    else:
        in_specs += [tok(d_ff), tok(d_ff)]
        args += list(conv_bounds)
        side_spec = tok(d_ff)
        side_shape = jax.ShapeDtypeStruct((rows, d_ff), F32)
    return pl.pallas_call(
        functools.partial(_post_kernel, tm=tm, d_ff=d_ff, seq_rows=seq_rows, carry_conv=carry_conv),
        grid=grid,
        in_specs=in_specs,
        out_specs=[tok(d_model), side_spec],
        out_shape=[jax.ShapeDtypeStruct((rows, d_model), F32), side_shape],
        scratch_shapes=[pltpu.VMEM((tm, d_model), BF16)],
        compiler_params=pltpu.CompilerParams(
            dimension_semantics=("arbitrary", "arbitrary"), vmem_limit_bytes=VMEM_LIMIT),
        name=name,
    )(*args)


def _rope_tables(pos):
    half = HEAD_DIM // 2
    inv = 1.0 / (ROPE_THETA ** (jnp.arange(half, dtype=F32) * (2.0 / HEAD_DIM)))
    ang = pos.astype(F32)[:, None] * inv[None, :]
    cos, sin = jnp.cos(ang), jnp.sin(ang)
    reps = LANES // HEAD_DIM
    return (jnp.tile(jnp.concatenate([cos, cos], axis=1), (1, reps)),
            jnp.tile(jnp.concatenate([-sin, sin], axis=1), (1, reps)))


def _pair_heads_cols(w_q):
    d = w_q.shape[0]
    per = A_Q_HEADS // A_KV_HEADS
    return w_q.reshape(d, A_KV_HEADS, per, HEAD_DIM).transpose(0, 2, 1, 3).reshape(d, A_Q_W)


def kernel(x_prompt, x_sample, cache_a_k, cache_a_v, cache_b_k, cache_b_v, state_conv, g_mix_pre, w_in, sinks,
           rel_bias, w_oa, w_ob, w_out, g_mix_post, g_ffn_pre, w_up, conv_w, conv_b, w_down, g_ffn_post):
    depth = w_in.shape[0]
    n_p, t_p, d_model = x_prompt.shape
    n_s, t_s, _ = x_sample.shape
    d_ff = w_down.shape[1]
    per = A_Q_HEADS // A_KV_HEADS

    cos_p, sin_p = _rope_tables(jnp.arange(t_p))
    cos_s, sin_s = _rope_tables(jnp.tile(PAST_LEN + jnp.arange(t_s), n_s))

    yp = x_prompt.reshape(n_p * t_p, d_model)
    ys = x_sample.reshape(n_s * t_s, d_model)
    new_p, new_s = [], []
    for l in range(depth):
        w_in_bf = jnp.concatenate([_pair_heads_cols(w_in[l][:, :A_Q_W]), w_in[l][:, A_Q_W:]], axis=1).astype(BF16)
        w = dict(
            w_oa=w_oa[l].reshape(A_KV_HEADS, per, HEAD_DIM, d_model).transpose(1, 0, 2, 3)
                        .reshape(A_Q_W, d_model).astype(BF16),
            w_ob=w_ob[l].astype(BF16), w_out=w_out[l].astype(BF16), g_mix_post=g_mix_post[l],
            g_ffn_pre=g_ffn_pre[l], w_up=_interleave_up(w_up[l], d_ff).astype(BF16), conv_w=conv_w[l],
            conv_b=conv_b[l],
            w_down=w_down[l].astype(BF16), g_ffn_post=g_ffn_post[l])
        sink_heads = (sinks[l].astype(F32) * LOG2E).reshape(A_KV_HEADS, per).T.reshape(N_PAIRS, 2)

        def sink_rows(n):
            return jnp.broadcast_to(jnp.repeat(sink_heads, n, axis=1)[:, :, None], (N_PAIRS, 2 * n, LANES))

        bias = _build_bias(rel_bias[l])
        far = (B_REACH - REL_CLIP) // LANES * LANES
        bias_p = bias[:, :, far:].reshape(N_PAIRS, 2 * CHUNK, B_BAND_PAD - far)
        keys_s = cache_b_k.shape[2] + t_s
        bias_s = bias[:, :t_s, :keys_s].reshape(N_PAIRS, 2 * t_s, keys_s)

        keep_a, keep_b = min(A_WINDOW, t_p), min(B_REACH, t_p)
        (qa, ka, va, qb, kb, vb, sga, sgb, ak, av, bk, bv) = _in_projection(
            yp, g_mix_pre[l], w_in_bf, cos_p, sin_p, n_seq=n_p, seq_rows=t_p, tm=min(IN_PROJ_TILE, t_p),
            keep_a=keep_a, keep_b=keep_b, name="prompt_in_projection")
        oa, ob = _prompt_attention(qa, ka, va, qb, kb, vb, bias_p, sink_rows(CHUNK),
                                   n_seq=n_p, seq_rows=t_p, tq=min(ATTN_TILE, t_p))
        yp, conv_p = _post(yp, oa, ob, sga, sgb, w, n_seq=n_p, seq_rows=t_p, tm=min(ROW_TILE, t_p),
                           name="prompt_post")
        new_p.append((ak.reshape(n_p, keep_a, A_KV_HEADS, HEAD_DIM), av.reshape(n_p, keep_a, A_KV_HEADS, HEAD_DIM),
                      bk.reshape(n_p, keep_b, B_HEADS, HEAD_DIM), bv.reshape(n_p, keep_b, B_HEADS, HEAD_DIM),
                      conv_p))

        rows_s = n_s * t_s
        (qa, ka, va, qb, kb, vb, sga, sgb, ak, av, bk, bv) = _in_projection(
            ys, g_mix_pre[l], w_in_bf, cos_s, sin_s, n_seq=1, seq_rows=rows_s, tm=rows_s,
            keep_a=rows_s, keep_b=rows_s, name="sample_in_projection")
        flat_heads = lambda c: c.reshape(c.shape[0], c.shape[1], -1).astype(BF16)
        oa, ob = _sample_attention(qa, ka, va, flat_heads(cache_a_k[l]), flat_heads(cache_a_v[l]),
                                   qb, kb, vb, flat_heads(cache_b_k[l]), flat_heads(cache_b_v[l]),
                                   bias_s, sink_rows(t_s), n_seq=n_s, seq_rows=t_s)
        st = state_conv[l].astype(F32)
        zeros = jnp.zeros((n_s, t_s - 2, d_ff), F32)
        prev1 = jnp.concatenate([st[:, 1:2], zeros, zeros[:, :1]], axis=1).reshape(rows_s, d_ff)
        prev2 = jnp.concatenate([st, zeros], axis=1).reshape(rows_s, d_ff)
        ys, u_s = _post(ys, oa, ob, sga, sgb, w, n_seq=n_s, seq_rows=t_s, tm=rows_s,
                        conv_bounds=(prev1, prev2), name="sample_post")
        new_s.append((ak.reshape(n_s, t_s, A_KV_HEADS, HEAD_DIM), av.reshape(n_s, t_s, A_KV_HEADS, HEAD_DIM),
                      bk.reshape(n_s, t_s, B_HEADS, HEAD_DIM), bv.reshape(n_s, t_s, B_HEADS, HEAD_DIM),
                      u_s.reshape(n_s, t_s, d_ff)[:, t_s - (CONV_WIDTH - 1):]))

    stack = lambda lst, k: jnp.stack([s[k] for s in lst], axis=0)
    return (yp.reshape(n_p, t_p, d_model), ys.reshape(n_s, t_s, d_model),
            stack(new_p, 0), stack(new_p, 1), stack(new_p, 2), stack(new_p, 3), stack(new_p, 4),
            stack(new_s, 0), stack(new_s, 1), stack(new_s, 2), stack(new_s, 3), stack(new_s, 4))
```
